```python
import jax, jax.numpy as jnp
from jax import lax
import numpy as np

D_MODEL = 2048
BATCH = 2
SEQ = 4096
DEPTH = 2
DEC_BATCH = 128
DEC_SEQ = 8
PAST_LEN = 8192
PAGE_SIZE = 128

F32 = jnp.float32
EPS = 1e-6

N_MIXERS = 2
N_A_LAYERS = (DEPTH + 1) // 2
N_B_LAYERS = DEPTH // 2

A_HEADS = 8
A_DK = D_MODEL // 16
A_DV = D_MODEL // 8
A_CHUNK = 64
GATE_SOFTCAP = 15.0
A_SPLITS = (A_HEADS * A_DK, 2 * A_HEADS * A_DK, 2 * A_HEADS * A_DK + A_HEADS * A_DV,
            2 * A_HEADS * A_DK + 2 * A_HEADS * A_DV)
A_IN = 2 * A_HEADS * A_DK + 2 * A_HEADS * A_DV + 2 * A_HEADS

B_HEADS = 32
B_KV_HEADS = 8
B_HD = D_MODEL // B_HEADS
B_GROUP = B_HEADS // B_KV_HEADS
WINDOW = 128
B_Q = B_HEADS * B_HD
B_KV = B_KV_HEADS * B_HD
B_IN = B_Q + 2 * B_KV

P_HEADS = 8
N_KEYS = 128
N_EXPERTS = N_KEYS * N_KEYS
P_QDIM = 256
P_HALF = P_QDIM // 2
P_TOPK = 16
P_BLOCK = 128

kernel_name = 'hybrid_mlstm_swa_peer_step'


def _rmsnorm(x, g):
    xf = x.astype(F32)
    y = xf * lax.rsqrt(jnp.mean(xf * xf, axis=-1, keepdims=True) + EPS)
    return (y * g.astype(F32)).astype(x.dtype)


def _mlstm_chunkwise(q, k, v, ig, logf, c0, n0, m0):
    bsz, nh, T, _ = q.shape
    L = A_CHUNK if T % A_CHUNK == 0 else T
    nc = T // L

    def to_chunks(a):
        return jnp.moveaxis(a.reshape(bsz, nh, nc, L, *a.shape[3:]), 2, 0)

    causal = jnp.tril(jnp.ones((L, L), dtype=bool))

    def step(carry, inp):
        c, n, m = carry
        qc, kc, vc, ic, fc = inp
        b = jnp.cumsum(fc, axis=-1)
        log_intra = jnp.where(causal, b[..., :, None] - b[..., None, :] + ic[..., None, :], -jnp.inf)
        log_inter = b + m[..., None]
        m_t = jnp.maximum(log_inter, jnp.max(log_intra, axis=-1))
        w_intra = jnp.exp(log_intra - m_t[..., None])
        w_inter = jnp.exp(log_inter - m_t)
        s = jnp.einsum('bhtd,bhsd->bhts', qc, kc) * w_intra
        num = w_inter[..., None] * jnp.einsum('bhtd,bhde->bhte', qc, c) + jnp.einsum('bhts,bhse->bhte', s, vc)
        den = w_inter * jnp.einsum('bhtd,bhd->bht', qc, n) + jnp.sum(s, axis=-1)
        h = num / jnp.maximum(jnp.abs(den), jnp.exp(-m_t))[..., None]
        b_end = b[..., -1]
        log_w = b_end[..., None] - b + ic
        m_new = jnp.maximum(b_end + m, jnp.max(log_w, axis=-1))
        decay = jnp.exp(b_end + m - m_new)
        w = jnp.exp(log_w - m_new[..., None])
        c_new = decay[..., None, None] * c + jnp.einsum('bhs,bhsd,bhse->bhde', w, kc, vc)
        n_new = decay[..., None] * n + jnp.einsum('bhs,bhsd->bhd', w, kc)
        return (c_new, n_new, m_new), h

    xs = tuple(to_chunks(a) for a in (q, k, v, ig, logf))
    (c, n, m), h = lax.scan(step, (c0, n0, m0), xs)
    h = jnp.moveaxis(h, 0, 2).reshape(bsz, nh, T, -1)
    return h, c, n, m


def _mlstm_mixer(xn, c0, n0, m0, w_in, b_gate, g_h, w_out):
    bsz, T, _ = xn.shape
    q, k, v, o, gates = jnp.split(xn @ w_in, A_SPLITS, axis=-1)

    def heads(a, d):
        return a.reshape(bsz, T, A_HEADS, d).transpose(0, 2, 1, 3).astype(F32)

    q = heads(q, A_DK)
    k = heads(k, A_DK) * (A_DK ** -0.5)
    v = heads(v, A_DV)
    gates = (gates + b_gate).astype(F32)
    gates = (GATE_SOFTCAP * jnp.tanh(gates / GATE_SOFTCAP)).transpose(0, 2, 1)
    ig = gates[:, :A_HEADS]
    logf = jax.nn.log_sigmoid(gates[:, A_HEADS:])
    h, c, n, m = _mlstm_chunkwise(q, k, v, ig, logf, c0.astype(F32), n0.astype(F32), m0.astype(F32))
    h = h * lax.rsqrt(jnp.mean(h * h, axis=-1, keepdims=True) + EPS)
    h = h.transpose(0, 2, 1, 3).reshape(bsz, T, A_HEADS * A_DV) * g_h.astype(F32)
    h = (h * jax.nn.sigmoid(o.astype(F32))).astype(xn.dtype)
    return h @ w_out, c, n, m


def _swa_project(xn, w_in, g_q, g_k):
    bsz, T, _ = xn.shape
    z = xn @ w_in
    q = z[..., :B_Q].reshape(bsz, T, B_KV_HEADS, B_GROUP, B_HD)
    k = z[..., B_Q:B_Q + B_KV].reshape(bsz, T, B_KV_HEADS, B_HD)
    v = z[..., B_Q + B_KV:].reshape(bsz, T, B_KV_HEADS, B_HD)
    return _rmsnorm(q, g_q), _rmsnorm(k, g_k), v


def _alibi_slopes():
    return 2.0 ** (-8.0 * (jnp.arange(B_HEADS, dtype=F32) + 1.0) / B_HEADS)


def _sink_attention(q, k, v, dist, sinks):
    s = jnp.einsum('...qkgd,...skd->...kgqs', q, k).astype(F32) * (B_HD ** -0.5)
    slopes = _alibi_slopes().reshape(B_KV_HEADS, B_GROUP, 1, 1)
    d = dist[..., None, None, :, :]
    valid = (d >= 0) & (d <= WINDOW)
    s = jnp.where(valid, s - slopes * d.astype(F32), -jnp.inf)
    sk = jnp.broadcast_to(sinks.astype(F32).reshape(B_KV_HEADS, B_GROUP, 1, 1), s.shape[:-1] + (1,))
    p = jax.nn.softmax(jnp.concatenate([s, sk], axis=-1), axis=-1)[..., :-1]
    return jnp.einsum('...kgqs,...skd->...qkgd', p.astype(v.dtype), v)


def _swa_prompt(xn, w_in, g_q, g_k, sinks, w_out):
    q, k, v = _swa_project(xn, w_in, g_q, g_k)
    bsz, T = xn.shape[:2]
    nb = T // WINDOW
    qb = q.reshape(bsz, nb, WINDOW, B_KV_HEADS, B_GROUP, B_HD)

    def band(a):
        ap = jnp.concatenate([jnp.zeros_like(a[:, :WINDOW]), a], axis=1)
        ap = ap.reshape(bsz, nb + 1, WINDOW, *a.shape[2:])
        return jnp.concatenate([ap[:, :-1], ap[:, 1:]], axis=2)

    blk = jnp.arange(nb)[:, None]
    q_pos = blk * WINDOW + jnp.arange(WINDOW)
    k_pos = (blk - 1) * WINDOW + jnp.arange(2 * WINDOW)
    dist = jnp.where(k_pos[:, None, :] >= 0, q_pos[:, :, None] - k_pos[:, None, :], -1)
    o = _sink_attention(qb, band(k), band(v), dist, sinks).reshape(bsz, T, B_Q)
    return o @ w_out, k[:, -WINDOW:], v[:, -WINDOW:]


def _swa_sample(xn, ck, cv, w_in, g_q, g_k, sinks, w_out):
    q, k, v = _swa_project(xn, w_in, g_q, g_k)
    bsz, T = xn.shape[:2]
    W = ck.shape[1]
    k_all = jnp.concatenate([ck.astype(k.dtype), k], axis=1)
    v_all = jnp.concatenate([cv.astype(v.dtype), v], axis=1)
    q_pos = PAST_LEN + jnp.arange(T)
    k_pos = PAST_LEN - W + jnp.arange(W + T)
    dist = q_pos[:, None] - k_pos[None, :]
    o = _sink_attention(q, k_all, v_all, dist, sinks).reshape(bsz, T, B_Q)
    return o @ w_out, k_all[:, -W:].astype(ck.dtype), v_all[:, -W:].astype(cv.dtype)


def _peer_ffn(x, w_q, sub_keys, u, v):
    lead = x.shape[:-1]
    xt = x.reshape(-1, D_MODEL)
    nt = xt.shape[0]
    q = (xt @ w_q).astype(F32).reshape(nt, P_HEADS, 2, P_HALF)
    s = jnp.einsum('thpc,hpkc->thpk', q, sub_keys.astype(F32))
    sv, si = lax.top_k(s, P_TOPK)
    cand = (sv[:, :, 0, :, None] + sv[:, :, 1, None, :]).reshape(nt, P_HEADS, P_TOPK * P_TOPK)
    cidx = (si[:, :, 0, :, None] * N_KEYS + si[:, :, 1, None, :]).reshape(nt, P_HEADS, P_TOPK * P_TOPK)
    fv, fi = lax.top_k(cand, P_TOPK)
    experts = jnp.take_along_axis(cidx, fi, axis=-1).reshape(nt, P_HEADS * P_TOPK)
    gates = jax.nn.softmax(fv, axis=-1).reshape(nt, P_HEADS * P_TOPK)
    pad = (-nt) % P_BLOCK
    nb = (nt + pad) // P_BLOCK
    xp = jnp.pad(xt, ((0, pad), (0, 0))).reshape(nb, P_BLOCK, D_MODEL)
    ep = jnp.pad(experts, ((0, pad), (0, 0))).reshape(nb, P_BLOCK, -1)
    gp = jnp.pad(gates, ((0, pad), (0, 0))).reshape(nb, P_BLOCK, -1)

    def block(args):
        xb, eb, gb = args
        hid = jnp.einsum('tkd,td->tk', u[eb], xb).astype(F32)
        a = (jax.nn.gelu(hid, approximate=False) * gb).astype(xb.dtype)
        return jnp.einsum('tk,tkd->td', a, v[eb])

    y = lax.map(block, (xp, ep, gp)).reshape(-1, D_MODEL)[:nt]
    return y.reshape(*lead, D_MODEL)


def setup_inputs(seed: int = 0) -> dict:
    key = jax.random.key(seed)
    ks = jax.random.split(key, 24)

    def nrm(k, shape, scale):
        return jax.random.normal(k, shape, F32) * scale

    w_buf = min(WINDOW, PAST_LEN)
    return {
        'x_prompt': nrm(ks[0], (BATCH, SEQ, D_MODEL), 1.0),
        'x_sample': nrm(ks[1], (DEC_BATCH, DEC_SEQ, D_MODEL), 1.0),
        'state_c': nrm(ks[2], (N_A_LAYERS, DEC_BATCH, A_HEADS, A_DK, A_DV), 0.1),
        'state_n': nrm(ks[3], (N_A_LAYERS, DEC_BATCH, A_HEADS, A_DK), 0.1),
        'state_m': nrm(ks[4], (N_A_LAYERS, DEC_BATCH, A_HEADS), 1.0),
        'cache_k': nrm(ks[5], (N_B_LAYERS, DEC_BATCH, w_buf, B_KV_HEADS, B_HD), 1.0),
        'cache_v': nrm(ks[6], (N_B_LAYERS, DEC_BATCH, w_buf, B_KV_HEADS, B_HD), 1.0),
        'norm_mix': 1.0 + nrm(ks[7], (DEPTH, D_MODEL), 0.02),
        'w_in_a': nrm(ks[8], (N_A_LAYERS, D_MODEL, A_IN), D_MODEL ** -0.5),
        'b_gate_a': jnp.concatenate([-2.0 + nrm(ks[9], (N_A_LAYERS, A_HEADS), 0.1),
                                     3.0 + nrm(ks[10], (N_A_LAYERS, A_HEADS), 0.5)], axis=-1),
        'norm_h_a': 1.0 + nrm(ks[11], (N_A_LAYERS, A_HEADS * A_DV), 0.02),
        'w_out_a': nrm(ks[12], (N_A_LAYERS, A_HEADS * A_DV, D_MODEL), (A_HEADS * A_DV) ** -0.5),
        'w_in_b': nrm(ks[13], (N_B_LAYERS, D_MODEL, B_IN), D_MODEL ** -0.5),
        'g_q_b': 1.0 + nrm(ks[14], (N_B_LAYERS, B_HD), 0.02),
        'g_k_b': 1.0 + nrm(ks[15], (N_B_LAYERS, B_HD), 0.02),
        'sink_b': nrm(ks[16], (N_B_LAYERS, B_HEADS), 1.0),
        'w_out_b': nrm(ks[17], (N_B_LAYERS, B_Q, D_MODEL), B_Q ** -0.5),
        'norm_ffn': 1.0 + nrm(ks[18], (DEPTH, D_MODEL), 0.02),
        'peer_wq': nrm(ks[19], (DEPTH, D_MODEL, P_HEADS * P_QDIM), D_MODEL ** -0.5),
        'peer_keys': nrm(ks[20], (DEPTH, P_HEADS, 2, N_KEYS, P_HALF), P_HALF ** -0.5),
        'peer_u': nrm(ks[21], (DEPTH, N_EXPERTS, D_MODEL), D_MODEL ** -0.5),
        'peer_v': nrm(ks[22], (DEPTH, N_EXPERTS, D_MODEL), P_HEADS ** -0.5),
    }


def reference(x_prompt, x_sample, state_c, state_n, state_m, cache_k, cache_v,
              norm_mix, w_in_a, b_gate_a, norm_h_a, w_out_a,
              w_in_b, g_q_b, g_k_b, sink_b, w_out_b,
              norm_ffn, peer_wq, peer_keys, peer_u, peer_v):
    xp, xs = x_prompt, x_sample
    bp = xp.shape[0]
    a_p, a_s, b_p, b_s = [], [], [], []
    for i in range(DEPTH):
        j = i // N_MIXERS
        hp = _rmsnorm(xp, norm_mix[i])
        hs = _rmsnorm(xs, norm_mix[i])
        if i % N_MIXERS == 0:
            c0 = jnp.zeros((bp, A_HEADS, A_DK, A_DV), F32)
            n0 = jnp.zeros((bp, A_HEADS, A_DK), F32)
            m0 = jnp.zeros((bp, A_HEADS), F32)
            yp, cp, np_, mp = _mlstm_mixer(hp, c0, n0, m0, w_in_a[j], b_gate_a[j], norm_h_a[j], w_out_a[j])
            ys, cs, ns, ms = _mlstm_mixer(hs, state_c[j], state_n[j], state_m[j],
                                          w_in_a[j], b_gate_a[j], norm_h_a[j], w_out_a[j])
            a_p.append((cp.astype(state_c.dtype), np_.astype(state_n.dtype), mp.astype(state_m.dtype)))
            a_s.append((cs.astype(state_c.dtype), ns.astype(state_n.dtype), ms.astype(state_m.dtype)))
        else:
            yp, kp, vp = _swa_prompt(hp, w_in_b[j], g_q_b[j], g_k_b[j], sink_b[j], w_out_b[j])
            ys, ks_, vs = _swa_sample(hs, cache_k[j], cache_v[j], w_in_b[j], g_q_b[j], g_k_b[j], sink_b[j], w_out_b[j])
            b_p.append((kp.astype(cache_k.dtype), vp.astype(cache_v.dtype)))
            b_s.append((ks_, vs))
        xp = xp + yp
        xs = xs + ys
        xp = xp + _peer_ffn(_rmsnorm(xp, norm_ffn[i]), peer_wq[i], peer_keys[i], peer_u[i], peer_v[i])
        xs = xs + _peer_ffn(_rmsnorm(xs, norm_ffn[i]), peer_wq[i], peer_keys[i], peer_u[i], peer_v[i])

    def stk(lst, idx):
        return jnp.stack([e[idx] for e in lst])

    return (xp, xs,
            stk(a_p, 0), stk(a_p, 1), stk(a_p, 2), stk(b_p, 0), stk(b_p, 1),
            stk(a_s, 0), stk(a_s, 1), stk(a_s, 2), stk(b_s, 0), stk(b_s, 1))
```

```python
import functools
import math

import jax
import jax.numpy as jnp
from jax import lax
from jax.experimental import pallas as pl
from jax.experimental.pallas import tpu as pltpu

F32 = jnp.float32
BF16 = jnp.bfloat16
EPS = 1e-6
NEG_INF = float("-inf")

A_HEADS = 8
A_DK = 128
A_DV = 256
GATE_SOFTCAP = 15.0
B_HEADS = 32
B_KV_HEADS = 8
B_HD = 64
B_GROUP = B_HEADS // B_KV_HEADS
WINDOW = 128
P_HEADS = 8
N_KEYS = 128
P_TOPK = 16

LANES = 128
SUBLANES = 8
VMEM_LIMIT = 56 * 1024 * 1024


def _params(*sem):
    return pltpu.CompilerParams(dimension_semantics=sem, vmem_limit_bytes=VMEM_LIMIT)


def _dot(a, b):
    return jnp.dot(a, b, preferred_element_type=F32)


def _dot_nt(a, b):
    return lax.dot_general(a, b, (((1,), (1,)), ((), ())), preferred_element_type=F32)


def _dot_tn(a, b):
    return lax.dot_general(a, b, (((0,), (0,)), ((), ())), preferred_element_type=F32)


def _rms(x, g):
    return x * lax.rsqrt(jnp.mean(x * x, axis=-1, keepdims=True) + EPS) * g


def _norm_matmul_kernel(x_ref, g_ref, w_ref, o_ref, xn_ref, *, precise):
    @pl.when(pl.program_id(1) == 0)
    def _():
        xn_ref[...] = _rms(x_ref[...], g_ref[...]).astype(xn_ref.dtype)

    if precise:
        o_ref[...] = jnp.dot(xn_ref[...], w_ref[...], preferred_element_type=F32,
                             precision=lax.Precision.HIGHEST)
    else:
        o_ref[...] = _dot(xn_ref[...], w_ref[...])


def _norm_matmul(x, g, w, *, tm, tn, precise=False):
    t, d = x.shape
    n = w.shape[1]
    return pl.pallas_call(
        functools.partial(_norm_matmul_kernel, precise=precise),
        grid=(t // tm, n // tn),
        in_specs=[pl.BlockSpec((tm, d), lambda i, j: (i, 0)),
                  pl.BlockSpec((1, d), lambda i, j: (0, 0)),
                  pl.BlockSpec((d, tn), lambda i, j: (0, j))],
        out_specs=pl.BlockSpec((tm, tn), lambda i, j: (i, j)),
        out_shape=jax.ShapeDtypeStruct((t, n), F32),
        scratch_shapes=[pltpu.VMEM((tm, d), F32 if precise else BF16)],
        compiler_params=_params("parallel", "arbitrary"),
        name="norm_matmul_f32" if precise else "norm_matmul",
    )(x, g.reshape(1, d), w)


def _matmul_res_kernel(a_ref, w_ref, r_ref, o_ref):
    o_ref[...] = r_ref[...] + _dot(a_ref[...], w_ref[...])


def _matmul_res(a, w, r, *, tm, tn):
    t, k = a.shape
    n = w.shape[1]
    return pl.pallas_call(
        _matmul_res_kernel,
        grid=(t // tm, n // tn),
        in_specs=[pl.BlockSpec((tm, k), lambda i, j: (i, 0)),
                  pl.BlockSpec((k, tn), lambda i, j: (0, j)),
                  pl.BlockSpec((tm, tn), lambda i, j: (i, j))],
        out_specs=pl.BlockSpec((tm, tn), lambda i, j: (i, j)),
        out_shape=jax.ShapeDtypeStruct((t, n), F32),
        compiler_params=_params("parallel", "arbitrary"),
        name="matmul_res",
    )(a, w, r)


def _log_sigmoid(x):
    return jnp.minimum(x, 0.0) - jnp.log(1.0 + jnp.exp(-jnp.abs(x)))


def _mx(x):
    return x.astype(BF16) if x.shape[0] % 16 == 0 else x


def _mlstm_kernel(*refs, chunk, n_sub, has_state):
    if has_state:
        (q_ref, k_ref, v_ref, o_ref, gc_ref, gr_ref, bc_ref, br_ref, gh_ref,
         c0_ref, n0_ref, m0_ref, _, h_ref, c_ref, n_ref, m_ref) = refs
    else:
        (q_ref, k_ref, v_ref, o_ref, gc_ref, gr_ref, bc_ref, br_ref, gh_ref,
         h_ref, c_ref, n_ref, m_ref) = refs
    L = chunk

    @pl.when(pl.program_id(1) == 0)
    def _():
        if has_state:
            c_ref[...] = c0_ref[...]
            n_ref[...] = n0_ref[...]
            m_ref[...] = m0_ref[...]
        else:
            c_ref[...] = jnp.zeros_like(c_ref)
            n_ref[...] = jnp.zeros_like(n_ref)
            m_ref[...] = jnp.zeros_like(m_ref)

    t_idx = lax.broadcasted_iota(jnp.int32, (L, L), 0)
    s_idx = lax.broadcasted_iota(jnp.int32, (L, L), 1)
    causal = s_idx <= t_idx
    heads_out = [[None] * n_sub for _ in range(A_HEADS)]

    for sub, h in [(sub, h) for sub in range(n_sub) for h in range(A_HEADS)]:
        r0, r1 = sub * L, (sub + 1) * L
        if h == 0:
            gcol = gc_ref[sub] + bc_ref[...]
            gcol = GATE_SOFTCAP * jnp.tanh(gcol / GATE_SOFTCAP)
            grow = gr_ref[sub] + br_ref[...]
            grow = GATE_SOFTCAP * jnp.tanh(grow / GATE_SOFTCAP)
            fcol_all = _log_sigmoid(gcol)
            frow_all = _log_sigmoid(grow)
        q = q_ref[r0:r1, h * A_DK:(h + 1) * A_DK]
        k = k_ref[r0:r1, h * A_DK:(h + 1) * A_DK] * (A_DK ** -0.5)
        v = v_ref[r0:r1, h * A_DV:(h + 1) * A_DV]
        i_col = gcol[:, h:h + 1]
        f_col = fcol_all[:, A_HEADS + h:A_HEADS + h + 1]
        i_row = grow[h:h + 1, :]
        f_row = frow_all[A_HEADS + h:A_HEADS + h + 1, :]
        c_prev = c_ref[sub, h]
        n_prev = n_ref[sub, h:h + 1, :]
        m_prev = m_ref[sub, h:h + 1, 0:1]

        b_col = jnp.sum(jnp.where(causal, f_row, 0.0), axis=1, keepdims=True)
        b_row = jnp.sum(jnp.where(t_idx <= s_idx, f_col, 0.0), axis=0, keepdims=True)

        log_intra = jnp.where(causal, b_col - b_row + i_row, NEG_INF)
        log_inter = b_col + m_prev
        m_t = jnp.maximum(log_inter, jnp.max(log_intra, axis=1, keepdims=True))
        w_intra = jnp.exp(log_intra - m_t)
        w_inter = jnp.exp(log_inter - m_t)

        qb = _mx(q)
        vb = _mx(v)
        s = _dot_nt(qb, _mx(k)) * w_intra
        num = w_inter * _dot(qb, c_prev.astype(qb.dtype)) + _dot(_mx(s), vb)
        den = w_inter * jnp.sum(q * n_prev, axis=1, keepdims=True) + jnp.sum(s, axis=1, keepdims=True)
        hh = num / jnp.maximum(jnp.abs(den), jnp.exp(-m_t))

        b_end = b_col[L - 1:L, :]
        log_w_row = b_end - b_row + i_row
        log_w_col = b_end - b_col + i_col
        m_new = jnp.maximum(b_end + m_prev, jnp.max(log_w_row, axis=1, keepdims=True))
        decay = jnp.exp(b_end + m_prev - m_new)
        kw = k * jnp.exp(log_w_col - m_new)
        c_ref[sub, h] = decay * c_prev + _dot_tn(_mx(kw), vb)
        n_ref[sub, h:h + 1, :] = decay * n_prev + jnp.sum(kw, axis=0, keepdims=True)
        m_ref[sub, h:h + 1, :] = jnp.broadcast_to(m_new, (1, LANES))

        hn = hh * lax.rsqrt(jnp.mean(hh * hh, axis=-1, keepdims=True) + EPS)
        hn = hn * gh_ref[:, h * A_DV:(h + 1) * A_DV]
        heads_out[h][sub] = hn * jax.nn.sigmoid(o_ref[r0:r1, h * A_DV:(h + 1) * A_DV])

    for h in range(A_HEADS):
        rows = heads_out[h][0] if n_sub == 1 else jnp.concatenate(heads_out[h], axis=0)
        h_ref[:, h * A_DV:(h + 1) * A_DV] = rows.astype(h_ref.dtype)


def _mlstm(z, zg, b_gate, g_h, *, row0, n_seq, seq_len, chunk, n_sub, t_total, state=None, h_prev=None):
    L = chunk
    nc = seq_len // L
    assert n_sub == 1 or nc == 1
    hd = A_HEADS * A_DK
    hv = A_HEADS * A_DV
    R = n_sub * L
    rb0 = row0 // R
    rows = n_seq * seq_len
    gates = zg[row0:row0 + rows, :2 * A_HEADS].reshape(n_seq * nc, L, 2 * A_HEADS)
    gates_t = jnp.swapaxes(gates, 1, 2)
    has_state = state is not None

    def rmap(col):
        return lambda s, c: (rb0 + s * nc + c, col)

    in_specs = [pl.BlockSpec((R, hd), rmap(0)),
                pl.BlockSpec((R, hd), rmap(1)),
                pl.BlockSpec((R, hv), rmap(hd * 2 // hv)),
                pl.BlockSpec((R, hv), rmap(hd * 2 // hv + 1)),
                pl.BlockSpec((n_sub, L, 2 * A_HEADS), lambda s, c: (s * nc + c, 0, 0)),
                pl.BlockSpec((n_sub, 2 * A_HEADS, L), lambda s, c: (s * nc + c, 0, 0)),
                pl.BlockSpec((1, 2 * A_HEADS), lambda s, c: (0, 0)),
                pl.BlockSpec((2 * A_HEADS, 1), lambda s, c: (0, 0)),
                pl.BlockSpec((1, hv), lambda s, c: (0, 0))]
    args = [z, z, z, z, gates, gates_t, b_gate.reshape(1, -1), b_gate.reshape(-1, 1), g_h.reshape(1, hv)]
    state_specs = [pl.BlockSpec((n_sub, A_HEADS, A_DK, A_DV), lambda s, c: (s, 0, 0, 0)),
                   pl.BlockSpec((n_sub, A_HEADS, A_DK), lambda s, c: (s, 0, 0)),
                   pl.BlockSpec((n_sub, A_HEADS, LANES), lambda s, c: (s, 0, 0))]
    aliases = {}
    if has_state:
        c0, n0, m0 = state
        in_specs += state_specs
        args += [c0, n0, jnp.broadcast_to(m0[..., None], m0.shape + (LANES,))]
        in_specs.append(pl.BlockSpec(memory_space=pl.ANY))
        args.append(h_prev)
        aliases = {len(args) - 1: 0}
    out_shape = (jax.ShapeDtypeStruct((t_total, hv), BF16),
                 jax.ShapeDtypeStruct((n_seq, A_HEADS, A_DK, A_DV), F32),
                 jax.ShapeDtypeStruct((n_seq, A_HEADS, A_DK), F32),
                 jax.ShapeDtypeStruct((n_seq, A_HEADS, LANES), F32))
    h, c, n, m = pl.pallas_call(
        functools.partial(_mlstm_kernel, chunk=L, n_sub=n_sub, has_state=has_state),
        grid=(n_seq // n_sub, nc),
        in_specs=in_specs,
        out_specs=(pl.BlockSpec((R, hv), rmap(0)),) + tuple(state_specs),
        out_shape=out_shape,
        input_output_aliases=aliases,
        compiler_params=_params("parallel", "arbitrary"),
        name="mlstm_state" if has_state else "mlstm",
    )(*args)
    return h, c, n, m[..., 0]


def _alibi_slope(h):
    return 2.0 ** (-8.0 * (h + 1.0) / B_HEADS)


def _sink_softmax(logits, sink):
    m = jnp.maximum(jnp.max(logits, axis=-1, keepdims=True), sink)
    p = jnp.exp(logits - m)
    return p / (jnp.sum(p, axis=-1, keepdims=True) + jnp.exp(sink - m))


def _swa_prompt_kernel(sink_ref, q_ref, kp_ref, ko_ref, vp_ref, vo_ref, gq_ref, gk_ref,
                       o_ref, kn_ref):
    blk = pl.program_id(1)
    W = WINDOW
    t_idx = lax.broadcasted_iota(jnp.int32, (W, 2 * W), 0)
    s_idx = lax.broadcasted_iota(jnp.int32, (W, 2 * W), 1)
    dist = W + t_idx - s_idx
    valid = (dist >= 0) & (dist <= W) & ((s_idx >= W) | (blk > 0))
    dist_f = dist.astype(F32)
    gq = gq_ref[...]
    gk = gk_ref[...]
    outs = []
    kns = []
    for kh in range(B_KV_HEADS):
        lo, hi = kh * B_HD, (kh + 1) * B_HD
        k_all = jnp.concatenate([kp_ref[:, lo:hi], ko_ref[:, lo:hi]], axis=0)
        v_all = jnp.concatenate([vp_ref[:, lo:hi], vo_ref[:, lo:hi]], axis=0)
        kn = _rms(k_all, gk)
        kns.append(kn[W:])
        knb = kn.astype(BF16)
        vb = v_all.astype(BF16)
        for g in range(B_GROUP):
            h = kh * B_GROUP + g
            qn = _rms(q_ref[:, h * B_HD:(h + 1) * B_HD], gq)
            s = _dot_nt(qn.astype(BF16), knb) * (B_HD ** -0.5)
            logits = jnp.where(valid, s - _alibi_slope(h) * dist_f, NEG_INF)
            p = _sink_softmax(logits, sink_ref[h])
            outs.append(_dot(p.astype(BF16), vb))
    o_ref[...] = jnp.concatenate(outs, axis=-1).astype(o_ref.dtype)
    kn_ref[0] = jnp.concatenate(kns, axis=-1)


def _swa_prompt(z, g_q, g_k, sinks, *, n_seq, seq_len, t_total):
    W = WINDOW
    nb = seq_len // W
    bq = B_HEADS * B_HD
    bkv = B_KV_HEADS * B_HD
    kcol = bq // bkv

    def own(col):
        return lambda b, i: (b * nb + i, col)

    def prev(col):
        return lambda b, i: (b * nb + jnp.maximum(i - 1, 0), col)

    return pl.pallas_call(
        _swa_prompt_kernel,
        grid=(n_seq, nb),
        in_specs=[pl.BlockSpec(memory_space=pltpu.SMEM),
                  pl.BlockSpec((W, bq), own(0)),
                  pl.BlockSpec((W, bkv), prev(kcol)),
                  pl.BlockSpec((W, bkv), own(kcol)),
                  pl.BlockSpec((W, bkv), prev(kcol + 1)),
                  pl.BlockSpec((W, bkv), own(kcol + 1)),
                  pl.BlockSpec((1, B_HD), lambda b, i: (0, 0)),
                  pl.BlockSpec((1, B_HD), lambda b, i: (0, 0))],
        out_specs=(pl.BlockSpec((W, bq), own(0)),
                   pl.BlockSpec((1, W, bkv), lambda b, i: (b, 0, 0))),
        out_shape=(jax.ShapeDtypeStruct((t_total, bq), BF16),
                   jax.ShapeDtypeStruct((n_seq, W, bkv), F32)),
        compiler_params=_params("parallel", "arbitrary"),
        name="swa_prompt",
    )(sinks, z, z, z, z, z, g_q.reshape(1, B_HD), g_k.reshape(1, B_HD))


def _swa_sample_kernel(sink_ref, q_ref, k_ref, v_ref, ck_ref, cv_ref, gq_ref, gk_ref, _,
                       o_ref, cko_ref, cvo_ref, *, n_new, n_batch):
    W = WINDOW
    T = n_new
    t_idx = lax.broadcasted_iota(jnp.int32, (T, W + T), 0)
    s_idx = lax.broadcasted_iota(jnp.int32, (T, W + T), 1)
    dist = W + t_idx - s_idx
    valid = (dist >= 0) & (dist <= W)
    dist_f = dist.astype(F32)
    gq = gq_ref[...]
    gk = gk_ref[...]
    rows_out = []
    for bi in range(n_batch):
        r0, r1 = bi * T, (bi + 1) * T
        outs = []
        for kh in range(B_KV_HEADS):
            lo, hi = kh * B_HD, (kh + 1) * B_HD
            kn_new = _rms(k_ref[r0:r1, lo:hi], gk)
            k_all = jnp.concatenate([ck_ref[bi, :, lo:hi], kn_new], axis=0)
            v_all = jnp.concatenate([cv_ref[bi, :, lo:hi], v_ref[r0:r1, lo:hi]], axis=0)
            cko_ref[bi, :, lo:hi] = k_all[T:]
            cvo_ref[bi, :, lo:hi] = v_all[T:]
            for g in range(B_GROUP):
                h = kh * B_GROUP + g
                qn = _rms(q_ref[r0:r1, h * B_HD:(h + 1) * B_HD], gq)
                s = _dot_nt(qn, k_all) * (B_HD ** -0.5)
                logits = jnp.where(valid, s - _alibi_slope(h) * dist_f, NEG_INF)
                p = _sink_softmax(logits, sink_ref[h])
                outs.append(_dot(p, v_all))
        rows_out.append(jnp.concatenate(outs, axis=-1))
    o_ref[...] = jnp.concatenate(rows_out, axis=0).astype(o_ref.dtype)


def _swa_sample(z, cache_k, cache_v, g_q, g_k, sinks, o_prev, *, row0, n_new, n_batch):
    W = WINDOW
    n_seq = cache_k.shape[0]
    bq = B_HEADS * B_HD
    bkv = B_KV_HEADS * B_HD
    kcol = bq // bkv
    rows = n_new * n_batch
    rb0 = row0 // rows
    ck = cache_k.reshape(n_seq, W, bkv)
    cv = cache_v.reshape(n_seq, W, bkv)
    cache_spec = pl.BlockSpec((n_batch, W, bkv), lambda i: (i, 0, 0))
    o, cko, cvo = pl.pallas_call(
        functools.partial(_swa_sample_kernel, n_new=n_new, n_batch=n_batch),
        grid=(n_seq // n_batch,),
        in_specs=[pl.BlockSpec(memory_space=pltpu.SMEM),
                  pl.BlockSpec((rows, bq), lambda i: (rb0 + i, 0)),
                  pl.BlockSpec((rows, bkv), lambda i: (rb0 + i, kcol)),
                  pl.BlockSpec((rows, bkv), lambda i: (rb0 + i, kcol + 1)),
                  cache_spec, cache_spec,
                  pl.BlockSpec((1, B_HD), lambda i: (0, 0)),
                  pl.BlockSpec((1, B_HD), lambda i: (0, 0)),
                  pl.BlockSpec(memory_space=pl.ANY)],
        out_specs=(pl.BlockSpec((rows, bq), lambda i: (rb0 + i, 0)), cache_spec, cache_spec),
        out_shape=(jax.ShapeDtypeStruct(o_prev.shape, o_prev.dtype),
                   jax.ShapeDtypeStruct(ck.shape, cache_k.dtype),
                   jax.ShapeDtypeStruct(cv.shape, cache_v.dtype)),
        input_output_aliases={8: 0},
        compiler_params=_params("parallel"),
        name="swa_sample",
    )(sinks, z, z, z, ck, cv, g_q.reshape(1, B_HD), g_k.reshape(1, B_HD), o_prev)
    return o, cko.reshape(cache_k.shape), cvo.reshape(cache_v.shape)


def _top_values(s, k):
    rows = s.shape[0]
    ridx = lax.broadcasted_iota(jnp.int32, s.shape, 0).astype(F32)
    vals = []
    for r in range(k):
        m = jnp.max(s, axis=0, keepdims=True)
        vals.append(m)
        if r + 1 < k:
            first = jnp.min(jnp.where(s == m, ridx, float(rows)), axis=0, keepdims=True)
            s = jnp.where(ridx == first, NEG_INF, s)
    return vals


_CAND_PAIRS = [(a, b) for a in range(P_TOPK) for b in range(P_TOPK) if (a + 1) * (b + 1) <= P_TOPK]
_CAND_ROWS = -(-len(_CAND_PAIRS) // 8) * 8


def _peer_route_kernel(x_ref, g_ref, wqt_ref, keys_ref, xnt_ref, s_ref, e_ref, thr_ref, cand_ref):
    xn = _rms(x_ref[...], g_ref[...])
    xnt = xn.T.astype(BF16)
    xnt_ref[...] = xnt
    qt = _dot(wqt_ref[...], xnt)
    half = keys_ref.shape[2]
    cand_ref[...] = jnp.full(cand_ref.shape, NEG_INF, F32)
    for h in range(P_HEADS):
        tops = []
        for p in range(2):
            hp = 2 * h + p
            s = _dot(keys_ref[hp], qt[hp * half:(hp + 1) * half, :])
            s_ref[hp] = s
            tops.append(_top_values(s, P_TOPK))
        for r, (a, b) in enumerate(_CAND_PAIRS):
            cand_ref[r:r + 1, :] = tops[0][a] + tops[1][b]
        best = _top_values(cand_ref[...], P_TOPK)
        z = jnp.zeros_like(best[0])
        for c in best:
            z = z + jnp.exp(c - best[0])
        thr_ref[h:h + 1, :] = best[P_TOPK - 1]
        e_ref[2 * h] = jnp.exp(s_ref[2 * h] - tops[0][0]) / z
        e_ref[2 * h + 1] = jnp.exp(s_ref[2 * h + 1] - tops[1][0])


def _peer_route(x, g, wqt, keys, *, tt):
    t, d = x.shape
    nq = wqt.shape[0]
    hp, nk, half = keys.shape
    tab = jax.ShapeDtypeStruct((hp, nk, t), F32)
    tab_spec = pl.BlockSpec((hp, nk, tt), lambda i: (0, 0, i))
    return pl.pallas_call(
        _peer_route_kernel,
        grid=(t // tt,),
        in_specs=[pl.BlockSpec((tt, d), lambda i: (i, 0)),
                  pl.BlockSpec((1, d), lambda i: (0, 0)),
                  pl.BlockSpec((nq, d), lambda i: (0, 0)),
                  pl.BlockSpec((hp, nk, half), lambda i: (0, 0, 0))],
        out_specs=(pl.BlockSpec((d, tt), lambda i: (0, i)), tab_spec, tab_spec,
                   pl.BlockSpec((P_HEADS, tt), lambda i: (0, i))),
        out_shape=(jax.ShapeDtypeStruct((d, t), BF16), tab, tab,
                   jax.ShapeDtypeStruct((P_HEADS, t), F32)),
        scratch_shapes=[pltpu.VMEM((_CAND_ROWS, tt), F32)],
        compiler_params=_params("parallel"),
        name="peer_route",
    )(x, g.reshape(1, d), wqt, keys)


def _gelu(x):
    return 0.5 * x * (1.0 + lax.erf(x * math.sqrt(0.5)))


def _peer_expert_kernel(xnt_ref, s_ref, e_ref, thr_ref, u_ref, vt_ref, yt_ref, hid_ref, a_ref, *, ne):
    et = pl.program_id(1)
    tt = hid_ref.shape[1]
    hid_ref[...] = _dot(u_ref[...], xnt_ref[...])
    n_i = ne // N_KEYS
    i_base = pl.multiple_of(et * n_i, SUBLANES)

    def body(ci, carry):
        cs = pl.ds(pl.multiple_of(ci * LANES, LANES), LANES)
        thr = [thr_ref[h:h + 1, cs] for h in range(P_HEADS)]
        for grp in range(n_i // SUBLANES):
            rows8 = pl.ds(i_base + grp * SUBLANES, SUBLANES)
            s0 = [s_ref[2 * h, rows8, cs] for h in range(P_HEADS)]
            e0 = [e_ref[2 * h, rows8, cs] for h in range(P_HEADS)]
            for r in range(SUBLANES):
                w = jnp.zeros((N_KEYS, LANES), F32)
                for h in range(P_HEADS):
                    c = s0[h][r:r + 1] + s_ref[2 * h + 1, :, cs]
                    w = w + jnp.where(c >= thr[h], e0[h][r:r + 1] * e_ref[2 * h + 1, :, cs], 0.0)
                rs = pl.ds((grp * SUBLANES + r) * N_KEYS, N_KEYS)
                a_ref[rs, cs] = (_gelu(hid_ref[rs, cs]) * w).astype(a_ref.dtype)
        return carry

    lax.fori_loop(0, tt // LANES, body, 0)
    contrib = _dot(vt_ref[...], a_ref[...])

    @pl.when(et == 0)
    def _():
        yt_ref[...] = contrib

    @pl.when(et > 0)
    def _():
        yt_ref[...] += contrib


def _peer_experts(xnt, s_tab, e_tab, thr, u, vt, *, tt, ne):
    d, t = xnt.shape
    hp, nk, _ = s_tab.shape
    n_exp = u.shape[0]
    tab_spec = pl.BlockSpec((hp, nk, tt), lambda i, e: (0, 0, i))
    return pl.pallas_call(
        functools.partial(_peer_expert_kernel, ne=ne),
        grid=(t // tt, n_exp // ne),
        in_specs=[pl.BlockSpec((d, tt), lambda i, e: (0, i)),
                  tab_spec, tab_spec,
                  pl.BlockSpec((P_HEADS, tt), lambda i, e: (0, i)),
                  pl.BlockSpec((ne, d), lambda i, e: (e, 0)),
                  pl.BlockSpec((d, ne), lambda i, e: (0, e))],
        out_specs=pl.BlockSpec((d, tt), lambda i, e: (0, i)),
        out_shape=jax.ShapeDtypeStruct((d, t), F32),
        scratch_shapes=[pltpu.VMEM((ne, tt), F32), pltpu.VMEM((ne, tt), BF16)],
        compiler_params=_params("parallel", "arbitrary"),
        name="peer_experts",
    )(xnt, s_tab, e_tab, thr, u, vt)


def _add_transposed_kernel(x_ref, yt_ref, o_ref):
    o_ref[...] = x_ref[...] + yt_ref[...].T


def _add_transposed(x, yt, *, tt):
    t, d = x.shape
    return pl.pallas_call(
        _add_transposed_kernel,
        grid=(t // tt,),
        in_specs=[pl.BlockSpec((tt, d), lambda i: (i, 0)),
                  pl.BlockSpec((d, tt), lambda i: (0, i))],
        out_specs=pl.BlockSpec((tt, d), lambda i: (i, 0)),
        out_shape=jax.ShapeDtypeStruct((t, d), F32),
        compiler_params=_params("parallel"),
        name="add_transposed",
    )(x, yt)


def _peer(x, g, w_q, keys, u, v, *, tt_route, tt, ne):
    hp = keys.shape[0] * keys.shape[1]
    xnt, s_tab, e_tab, thr = _peer_route(x, g, w_q.T.astype(BF16),
                                         keys.reshape(hp, keys.shape[2], keys.shape[3]), tt=tt_route)
    yt = _peer_experts(xnt, s_tab, e_tab, thr, u.astype(BF16), v.T.astype(BF16), tt=tt, ne=ne)
    return _add_transposed(x, yt, tt=tt_route)


def _pick_tile(n, pref):
    t = pref
    while n % t:
        t //= 2
    return t


def kernel(x_prompt, x_sample, state_c, state_n, state_m, cache_k, cache_v, norm_mix, w_in_a, b_gate_a,
           norm_h_a, w_out_a, w_in_b, g_q_b, g_k_b, sink_b, w_out_b, norm_ffn, peer_wq, peer_keys,
           peer_u, peer_v):
    bp, sp, d = x_prompt.shape
    bs, ss, _ = x_sample.shape
    tp, ts = bp * sp, bs * ss
    t = tp + ts
    x = jnp.concatenate([x_prompt.reshape(tp, d), x_sample.reshape(ts, d)], axis=0)
    tm = _pick_tile(math.gcd(tp, ts), 512)
    tr = min(tm, 256)
    n_qkvo = 2 * A_HEADS * (A_DK + A_DV)
    chunk_p = _pick_tile(sp, 256)

    w_in = w_in_a[0]
    z = _norm_matmul(x, norm_mix[0], w_in[:, :n_qkvo].astype(BF16), tm=tm, tn=1024)
    w_gate = jnp.pad(w_in[:, n_qkvo:], ((0, 0), (0, LANES - 2 * A_HEADS)))
    zg = _norm_matmul(x, norm_mix[0], w_gate, tm=tm, tn=LANES, precise=True)
    h, c_p, n_p, m_p = _mlstm(z, zg, b_gate_a[0], norm_h_a[0], row0=0, n_seq=bp, seq_len=sp,
                              chunk=chunk_p, n_sub=1, t_total=t)
    h, c_s, n_s, m_s = _mlstm(z, zg, b_gate_a[0], norm_h_a[0], row0=tp, n_seq=bs, seq_len=ss,
                              chunk=ss, n_sub=2, t_total=t,
                              state=(state_c[0], state_n[0], state_m[0]), h_prev=h)
    x = _matmul_res(h, w_out_a[0].astype(BF16), x, tm=tm, tn=1024)
    x = _peer(x, norm_ffn[0], peer_wq[0], peer_keys[0], peer_u[0], peer_v[0], tt_route=tr, tt=tm, ne=1024)

    z = _norm_matmul(x, norm_mix[1], w_in_b[0].astype(BF16), tm=tm, tn=1024)
    o, kn_p = _swa_prompt(z, g_q_b[0], g_k_b[0], sink_b[0], n_seq=bp, seq_len=sp, t_total=t)
    o, ck_s, cv_s = _swa_sample(z, cache_k[0], cache_v[0], g_q_b[0], g_k_b[0], sink_b[0], o,
                                row0=tp, n_new=ss, n_batch=2)
    x = _matmul_res(o, w_out_b[0].astype(BF16), x, tm=tm, tn=1024)
    x = _peer(x, norm_ffn[1], peer_wq[1], peer_keys[1], peer_u[1], peer_v[1], tt_route=tr, tt=tm, ne=1024)

    bq = B_HEADS * B_HD
    bkv = B_KV_HEADS * B_HD
    v_p = z[:tp, bq + bkv:].reshape(bp, sp, bkv)[:, sp - WINDOW:]
    cache_shape = (1, bp, WINDOW, B_KV_HEADS, B_HD)
    return (x[:tp].reshape(bp, sp, d), x[tp:].reshape(bs, ss, d),
            c_p[None], n_p[None], m_p[None],
            kn_p.reshape(cache_shape), v_p.reshape(cache_shape),
            c_s[None], n_s[None], m_s[None], ck_s[None], cv_s[None])
```

```python
import functools
import math

import jax
import jax.numpy as jnp
from jax import lax
from jax.experimental import pallas as pl
from jax.experimental.pallas import tpu as pltpu

F32 = jnp.float32
BF16 = jnp.bfloat16
EPS = 1e-6
NEG_INF = float("-inf")

A_HEADS = 8
A_DK = 128
A_DV = 256
GATE_SOFTCAP = 15.0
B_HEADS = 32
B_KV_HEADS = 8
B_HD = 64
B_GROUP = B_HEADS // B_KV_HEADS
WINDOW = 128
P_HEADS = 8
N_KEYS = 128
P_TOPK = 16

LANES = 128
SUBLANES = 8
BF16_SUBLANES = 16
VMEM_LIMIT = 56 * 1024 * 1024


def _params(*sem):
    return pltpu.CompilerParams(dimension_semantics=sem, vmem_limit_bytes=VMEM_LIMIT)


def _dot(a, b):
    return jnp.dot(a, b, preferred_element_type=F32)


def _dot_nt(a, b):
    return lax.dot_general(a, b, (((1,), (1,)), ((), ())), preferred_element_type=F32)


def _dot_tn(a, b):
    return lax.dot_general(a, b, (((0,), (0,)), ((), ())), preferred_element_type=F32)


def _rms(x, g):
    return x * lax.rsqrt(jnp.mean(x * x, axis=-1, keepdims=True) + EPS) * g


def _norm_matmul_kernel(x_ref, g_ref, w_ref, o_ref, xn_ref, *, precise):
    @pl.when(pl.program_id(1) == 0)
    def _():
        xn_ref[...] = _rms(x_ref[...], g_ref[...]).astype(xn_ref.dtype)

    if precise:
        o_ref[...] = jnp.dot(xn_ref[...], w_ref[...], preferred_element_type=F32,
                             precision=lax.Precision.HIGHEST)
    else:
        o_ref[...] = _dot(xn_ref[...], w_ref[...])


def _norm_matmul(x, g, w, *, tm, tn, precise=False):
    t, d = x.shape
    n = w.shape[1]
    return pl.pallas_call(
        functools.partial(_norm_matmul_kernel, precise=precise),
        grid=(t // tm, n // tn),
        in_specs=[pl.BlockSpec((tm, d), lambda i, j: (i, 0)),
                  pl.BlockSpec((1, d), lambda i, j: (0, 0)),
                  pl.BlockSpec((d, tn), lambda i, j: (0, j))],
        out_specs=pl.BlockSpec((tm, tn), lambda i, j: (i, j)),
        out_shape=jax.ShapeDtypeStruct((t, n), F32),
        scratch_shapes=[pltpu.VMEM((tm, d), F32 if precise else BF16)],
        compiler_params=_params("parallel", "arbitrary"),
        name="norm_matmul_f32" if precise else "norm_matmul",
    )(x, g.reshape(1, d), w)


def _matmul_res_kernel(a_ref, w_ref, r_ref, o_ref):
    o_ref[...] = r_ref[...] + _dot(a_ref[...], w_ref[...])


def _matmul_res(a, w, r, *, tm, tn):
    t, k = a.shape
    n = w.shape[1]
    return pl.pallas_call(
        _matmul_res_kernel,
        grid=(t // tm, n // tn),
        in_specs=[pl.BlockSpec((tm, k), lambda i, j: (i, 0)),
                  pl.BlockSpec((k, tn), lambda i, j: (0, j)),
                  pl.BlockSpec((tm, tn), lambda i, j: (i, j))],
        out_specs=pl.BlockSpec((tm, tn), lambda i, j: (i, j)),
        out_shape=jax.ShapeDtypeStruct((t, n), F32),
        compiler_params=_params("parallel", "arbitrary"),
        name="matmul_res",
    )(a, w, r)


def _log_sigmoid(x):
    return jnp.minimum(x, 0.0) - jnp.log(1.0 + jnp.exp(-jnp.abs(x)))


def _mx(x):
    return x.astype(BF16) if x.shape[0] % 16 == 0 else x


def _mlstm_kernel(*refs, chunk, n_sub, has_state):
    if has_state:
        (q_ref, k_ref, v_ref, o_ref, gc_ref, gr_ref, bc_ref, br_ref, gh_ref,
         c0_ref, n0_ref, m0_ref, _, h_ref, c_ref, n_ref, m_ref) = refs
    else:
        (q_ref, k_ref, v_ref, o_ref, gc_ref, gr_ref, bc_ref, br_ref, gh_ref,
         h_ref, c_ref, n_ref, m_ref) = refs
    L = chunk

    @pl.when(pl.program_id(1) == 0)
    def _():
        if has_state:
            c_ref[...] = c0_ref[...]
            n_ref[...] = n0_ref[...]
            m_ref[...] = m0_ref[...]
        else:
            c_ref[...] = jnp.zeros_like(c_ref)
            n_ref[...] = jnp.zeros_like(n_ref)
            m_ref[...] = jnp.zeros_like(m_ref)

    t_idx = lax.broadcasted_iota(jnp.int32, (L, L), 0)
    s_idx = lax.broadcasted_iota(jnp.int32, (L, L), 1)
    causal = s_idx <= t_idx
    heads_out = [[None] * n_sub for _ in range(A_HEADS)]
    units = [(sub, h) for sub in range(n_sub) for h in range(A_HEADS)]
    stage = []

    for sub, h in units:
        r0, r1 = sub * L, (sub + 1) * L
        if h == 0:
            gcol = gc_ref[sub] + bc_ref[...]
            gcol = GATE_SOFTCAP * jnp.tanh(gcol / GATE_SOFTCAP)
            grow = gr_ref[sub] + br_ref[...]
            grow = GATE_SOFTCAP * jnp.tanh(grow / GATE_SOFTCAP)
            fcol_all = _log_sigmoid(gcol)
            frow_all = _log_sigmoid(grow)
        q = q_ref[r0:r1, h * A_DK:(h + 1) * A_DK]
        k = k_ref[r0:r1, h * A_DK:(h + 1) * A_DK] * (A_DK ** -0.5)
        v = v_ref[r0:r1, h * A_DV:(h + 1) * A_DV]
        i_col = gcol[:, h:h + 1]
        f_col = fcol_all[:, A_HEADS + h:A_HEADS + h + 1]
        i_row = grow[h:h + 1, :]
        f_row = frow_all[A_HEADS + h:A_HEADS + h + 1, :]
        c_prev = c_ref[sub, h]
        n_prev = n_ref[sub, h:h + 1, :]
        m_prev = m_ref[sub, h:h + 1, 0:1]

        b_col = jnp.sum(jnp.where(causal, f_row, 0.0), axis=1, keepdims=True)
        b_row = jnp.sum(jnp.where(t_idx <= s_idx, f_col, 0.0), axis=0, keepdims=True)

        log_intra = jnp.where(causal, b_col - b_row + i_row, NEG_INF)
        log_inter = b_col + m_prev
        m_t = jnp.maximum(log_inter, jnp.max(log_intra, axis=1, keepdims=True))
        w_intra = jnp.exp(log_intra - m_t)
        w_inter = jnp.exp(log_inter - m_t)

        b_end = b_col[L - 1:L, :]
        log_w_row = b_end - b_row + i_row
        log_w_col = b_end - b_col + i_col
        m_new = jnp.maximum(b_end + m_prev, jnp.max(log_w_row, axis=1, keepdims=True))
        decay = jnp.exp(b_end + m_prev - m_new)
        kw = k * jnp.exp(log_w_col - m_new)

        qb = _mx(q)
        vb = _mx(v)
        stage.append(dict(
            s=_dot_nt(qb, _mx(k)) * w_intra,
            qc=_dot(qb, c_prev.astype(qb.dtype)),
            kv=_dot_tn(_mx(kw), vb),
            qn=jnp.sum(q * n_prev, axis=1, keepdims=True),
            kw_sum=jnp.sum(kw, axis=0, keepdims=True),
            vb=vb, w_inter=w_inter, m_t=m_t, m_new=m_new, decay=decay, c_prev=c_prev, n_prev=n_prev))

    for st in stage:
        st["sv"] = _dot(_mx(st["s"]), st["vb"])

    for (sub, h), st in zip(units, stage):
        r0, r1 = sub * L, (sub + 1) * L
        num = st["w_inter"] * st["qc"] + st["sv"]
        den = st["w_inter"] * st["qn"] + jnp.sum(st["s"], axis=1, keepdims=True)
        hh = num / jnp.maximum(jnp.abs(den), jnp.exp(-st["m_t"]))
        c_ref[sub, h] = st["decay"] * st["c_prev"] + st["kv"]
        n_ref[sub, h:h + 1, :] = st["decay"] * st["n_prev"] + st["kw_sum"]
        m_ref[sub, h:h + 1, :] = jnp.broadcast_to(st["m_new"], (1, LANES))

        hn = hh * lax.rsqrt(jnp.mean(hh * hh, axis=-1, keepdims=True) + EPS)
        hn = hn * gh_ref[:, h * A_DV:(h + 1) * A_DV]
        heads_out[h][sub] = hn * jax.nn.sigmoid(o_ref[r0:r1, h * A_DV:(h + 1) * A_DV])

    for h in range(A_HEADS):
        rows = heads_out[h][0] if n_sub == 1 else jnp.concatenate(heads_out[h], axis=0)
        h_ref[:, h * A_DV:(h + 1) * A_DV] = rows.astype(h_ref.dtype)


def _mlstm(z, zg, b_gate, g_h, *, row0, n_seq, seq_len, chunk, n_sub, t_total, state=None, h_prev=None):
    L = chunk
    nc = seq_len // L
    assert n_sub == 1 or nc == 1
    hd = A_HEADS * A_DK
    hv = A_HEADS * A_DV
    R = n_sub * L
    rb0 = row0 // R
    rows = n_seq * seq_len
    gates = zg[row0:row0 + rows, :2 * A_HEADS].reshape(n_seq * nc, L, 2 * A_HEADS)
    gates_t = jnp.swapaxes(gates, 1, 2)
    has_state = state is not None

    def rmap(col):
        return lambda s, c: (rb0 + s * nc + c, col)

    in_specs = [pl.BlockSpec((R, hd), rmap(0)),
                pl.BlockSpec((R, hd), rmap(1)),
                pl.BlockSpec((R, hv), rmap(hd * 2 // hv)),
                pl.BlockSpec((R, hv), rmap(hd * 2 // hv + 1)),
                pl.BlockSpec((n_sub, L, 2 * A_HEADS), lambda s, c: (s * nc + c, 0, 0)),
                pl.BlockSpec((n_sub, 2 * A_HEADS, L), lambda s, c: (s * nc + c, 0, 0)),
                pl.BlockSpec((1, 2 * A_HEADS), lambda s, c: (0, 0)),
                pl.BlockSpec((2 * A_HEADS, 1), lambda s, c: (0, 0)),
                pl.BlockSpec((1, hv), lambda s, c: (0, 0))]
    args = [z, z, z, z, gates, gates_t, b_gate.reshape(1, -1), b_gate.reshape(-1, 1), g_h.reshape(1, hv)]
    state_specs = [pl.BlockSpec((n_sub, A_HEADS, A_DK, A_DV), lambda s, c: (s, 0, 0, 0)),
                   pl.BlockSpec((n_sub, A_HEADS, A_DK), lambda s, c: (s, 0, 0)),
                   pl.BlockSpec((n_sub, A_HEADS, LANES), lambda s, c: (s, 0, 0))]
    aliases = {}
    if has_state:
        c0, n0, m0 = state
        in_specs += state_specs
        args += [c0, n0, jnp.broadcast_to(m0[..., None], m0.shape + (LANES,))]
        in_specs.append(pl.BlockSpec(memory_space=pl.ANY))
        args.append(h_prev)
        aliases = {len(args) - 1: 0}
    out_shape = (jax.ShapeDtypeStruct((t_total, hv), BF16),
                 jax.ShapeDtypeStruct((n_seq, A_HEADS, A_DK, A_DV), F32),
                 jax.ShapeDtypeStruct((n_seq, A_HEADS, A_DK), F32),
                 jax.ShapeDtypeStruct((n_seq, A_HEADS, LANES), F32))
    h, c, n, m = pl.pallas_call(
        functools.partial(_mlstm_kernel, chunk=L, n_sub=n_sub, has_state=has_state),
        grid=(n_seq // n_sub, nc),
        in_specs=in_specs,
        out_specs=(pl.BlockSpec((R, hv), rmap(0)),) + tuple(state_specs),
        out_shape=out_shape,
        input_output_aliases=aliases,
        compiler_params=_params("parallel", "arbitrary"),
        name="mlstm_state" if has_state else "mlstm",
    )(*args)
    return h, c, n, m[..., 0]


def _alibi_slope(h):
    return 2.0 ** (-8.0 * (h + 1.0) / B_HEADS)


def _sink_softmax(logits, sink):
    m = jnp.maximum(jnp.max(logits, axis=-1, keepdims=True), sink)
    p = jnp.exp(logits - m)
    return p / (jnp.sum(p, axis=-1, keepdims=True) + jnp.exp(sink - m))


def _swa_prompt_kernel(sink_ref, q_ref, kp_ref, ko_ref, vp_ref, vo_ref, gq_ref, gk_ref,
                       o_ref, kn_ref):
    blk = pl.program_id(1)
    W = WINDOW
    t_idx = lax.broadcasted_iota(jnp.int32, (W, 2 * W), 0)
    s_idx = lax.broadcasted_iota(jnp.int32, (W, 2 * W), 1)
    dist = W + t_idx - s_idx
    valid = (dist >= 0) & (dist <= W) & ((s_idx >= W) | (blk > 0))
    dist_f = dist.astype(F32)
    gq = gq_ref[...]
    gk = gk_ref[...]
    qs, ks, vs, kns = [], [], [], []
    for kh in range(B_KV_HEADS):
        lo, hi = kh * B_HD, (kh + 1) * B_HD
        k_all = jnp.concatenate([kp_ref[:, lo:hi], ko_ref[:, lo:hi]], axis=0)
        v_all = jnp.concatenate([vp_ref[:, lo:hi], vo_ref[:, lo:hi]], axis=0)
        kn = _rms(k_all, gk)
        kns.append(kn[W:])
        ks.append(kn.astype(BF16))
        vs.append(v_all.astype(BF16))
        heads = range(kh * B_GROUP, (kh + 1) * B_GROUP)
        qs.append(jnp.concatenate([_rms(q_ref[:, h * B_HD:(h + 1) * B_HD], gq) for h in heads],
                                  axis=0).astype(BF16))
    scores = [_dot_nt(qs[kh], ks[kh]) for kh in range(B_KV_HEADS)]
    probs = []
    for kh in range(B_KV_HEADS):
        pieces = []
        for g in range(B_GROUP):
            h = kh * B_GROUP + g
            s = scores[kh][g * W:(g + 1) * W] * (B_HD ** -0.5)
            logits = jnp.where(valid, s - _alibi_slope(h) * dist_f, NEG_INF)
            pieces.append(_sink_softmax(logits, sink_ref[h]).astype(BF16))
        probs.append(jnp.concatenate(pieces, axis=0))
    outs = [_dot(probs[kh], vs[kh]) for kh in range(B_KV_HEADS)]
    o_ref[...] = jnp.concatenate(
        [outs[kh][g * W:(g + 1) * W] for kh in range(B_KV_HEADS) for g in range(B_GROUP)],
        axis=-1).astype(o_ref.dtype)
    kn_ref[0] = jnp.concatenate(kns, axis=-1)


def _swa_prompt(z, g_q, g_k, sinks, *, n_seq, seq_len, t_total):
    W = WINDOW
    nb = seq_len // W
    bq = B_HEADS * B_HD
    bkv = B_KV_HEADS * B_HD
    kcol = bq // bkv

    def own(col):
        return lambda b, i: (b * nb + i, col)

    def prev(col):
        return lambda b, i: (b * nb + jnp.maximum(i - 1, 0), col)

    return pl.pallas_call(
        _swa_prompt_kernel,
        grid=(n_seq, nb),
        in_specs=[pl.BlockSpec(memory_space=pltpu.SMEM),
                  pl.BlockSpec((W, bq), own(0)),
                  pl.BlockSpec((W, bkv), prev(kcol)),
                  pl.BlockSpec((W, bkv), own(kcol)),
                  pl.BlockSpec((W, bkv), prev(kcol + 1)),
                  pl.BlockSpec((W, bkv), own(kcol + 1)),
                  pl.BlockSpec((1, B_HD), lambda b, i: (0, 0)),
                  pl.BlockSpec((1, B_HD), lambda b, i: (0, 0))],
        out_specs=(pl.BlockSpec((W, bq), own(0)),
                   pl.BlockSpec((1, W, bkv), lambda b, i: (b, 0, 0))),
        out_shape=(jax.ShapeDtypeStruct((t_total, bq), BF16),
                   jax.ShapeDtypeStruct((n_seq, W, bkv), F32)),
        compiler_params=_params("parallel", "arbitrary"),
        name="swa_prompt",
    )(sinks, z, z, z, z, z, g_q.reshape(1, B_HD), g_k.reshape(1, B_HD))


def _swa_sample_kernel(sink_ref, q_ref, k_ref, v_ref, ck_ref, cv_ref, gq_ref, gk_ref, _,
                       o_ref, cko_ref, cvo_ref, *, n_new, n_batch):
    W = WINDOW
    T = n_new
    R = B_HEADS * T
    GT = B_GROUP * T
    r_col = lax.broadcasted_iota(jnp.int32, (R, 1), 0)
    s_idx = lax.broadcasted_iota(jnp.int32, (R, W + T), 1)
    t_col = r_col
    slope_col = jnp.full((R, 1), _alibi_slope(0), F32)
    sink_col = jnp.full((R, 1), sink_ref[0], F32)
    for h in range(1, B_HEADS):
        in_later_head = r_col >= h * T
        t_col = jnp.where(in_later_head, r_col - h * T, t_col)
        slope_col = jnp.where(in_later_head, _alibi_slope(h), slope_col)
        sink_col = jnp.where(in_later_head, sink_ref[h], sink_col)
    dist = W + t_col - s_idx
    valid = (dist >= 0) & (dist <= W)
    bias = slope_col * dist.astype(F32)
    gq = gq_ref[...]
    gk = gk_ref[...]
    qs, ks, vs = [], [], []
    for bi in range(n_batch):
        r0, r1 = bi * T, (bi + 1) * T
        for kh in range(B_KV_HEADS):
            lo, hi = kh * B_HD, (kh + 1) * B_HD
            kn_new = _rms(k_ref[r0:r1, lo:hi], gk)
            k_all = jnp.concatenate([ck_ref[bi, :, lo:hi], kn_new], axis=0)
            v_all = jnp.concatenate([cv_ref[bi, :, lo:hi], v_ref[r0:r1, lo:hi]], axis=0)
            cko_ref[bi, :, lo:hi] = k_all[T:]
            cvo_ref[bi, :, lo:hi] = v_all[T:]
            ks.append(k_all)
            vs.append(v_all)
            heads = range(kh * B_GROUP, (kh + 1) * B_GROUP)
            qs.append(jnp.concatenate(
                [_rms(q_ref[r0:r1, h * B_HD:(h + 1) * B_HD], gq) for h in heads], axis=0))
    scores = [_dot_nt(q, k) for q, k in zip(qs, ks)]
    probs = []
    for bi in range(n_batch):
        s = jnp.concatenate(scores[bi * B_KV_HEADS:(bi + 1) * B_KV_HEADS], axis=0)
        logits = jnp.where(valid, s * (B_HD ** -0.5) - bias, NEG_INF)
        probs.append(_sink_softmax(logits, sink_col))
    outs = [_dot(probs[i // B_KV_HEADS][(i % B_KV_HEADS) * GT:(i % B_KV_HEADS + 1) * GT], v)
            for i, v in enumerate(vs)]
    rows_out = []
    for bi in range(n_batch):
        rows_out.append(jnp.concatenate(
            [outs[bi * B_KV_HEADS + kh][g * T:(g + 1) * T]
             for kh in range(B_KV_HEADS) for g in range(B_GROUP)], axis=-1))
    o_ref[...] = jnp.concatenate(rows_out, axis=0).astype(o_ref.dtype)


def _swa_sample(z, cache_k, cache_v, g_q, g_k, sinks, o_prev, *, row0, n_new, n_batch):
    W = WINDOW
    n_seq = cache_k.shape[0]
    bq = B_HEADS * B_HD
    bkv = B_KV_HEADS * B_HD
    kcol = bq // bkv
    rows = n_new * n_batch
    rb0 = row0 // rows
    ck = cache_k.reshape(n_seq, W, bkv)
    cv = cache_v.reshape(n_seq, W, bkv)
    cache_spec = pl.BlockSpec((n_batch, W, bkv), lambda i: (i, 0, 0))
    o, cko, cvo = pl.pallas_call(
        functools.partial(_swa_sample_kernel, n_new=n_new, n_batch=n_batch),
        grid=(n_seq // n_batch,),
        in_specs=[pl.BlockSpec(memory_space=pltpu.SMEM),
                  pl.BlockSpec((rows, bq), lambda i: (rb0 + i, 0)),
                  pl.BlockSpec((rows, bkv), lambda i: (rb0 + i, kcol)),
                  pl.BlockSpec((rows, bkv), lambda i: (rb0 + i, kcol + 1)),
                  cache_spec, cache_spec,
                  pl.BlockSpec((1, B_HD), lambda i: (0, 0)),
                  pl.BlockSpec((1, B_HD), lambda i: (0, 0)),
                  pl.BlockSpec(memory_space=pl.ANY)],
        out_specs=(pl.BlockSpec((rows, bq), lambda i: (rb0 + i, 0)), cache_spec, cache_spec),
        out_shape=(jax.ShapeDtypeStruct(o_prev.shape, o_prev.dtype),
                   jax.ShapeDtypeStruct(ck.shape, cache_k.dtype),
                   jax.ShapeDtypeStruct(cv.shape, cache_v.dtype)),
        input_output_aliases={8: 0},
        compiler_params=_params("parallel"),
        name="swa_sample",
    )(sinks, z, z, z, ck, cv, g_q.reshape(1, B_HD), g_k.reshape(1, B_HD), o_prev)
    return o, cko.reshape(cache_k.shape), cvo.reshape(cache_v.shape)


def _top_ranked(s, k):
    rows = s.shape[0]
    ridx = lax.broadcasted_iota(jnp.int32, s.shape, 0).astype(F32)
    rank = jnp.full(s.shape, float(k), F32)
    vals = []
    for r in range(k):
        m = jnp.max(s, axis=0, keepdims=True)
        vals.append(m)
        first = ridx == jnp.min(jnp.where(s == m, ridx, float(rows)), axis=0, keepdims=True)
        rank = jnp.where(first, float(r), rank)
        s = jnp.where(first, NEG_INF, s)
    return vals, rank


_CAND_PAIRS = [(a, b) for a in range(P_TOPK) for b in range(P_TOPK) if (a + 1) * (b + 1) <= P_TOPK]
_CAND_ROWS = -(-len(_CAND_PAIRS) // 8) * 8


TAB_COUNT0, TAB_GATE0 = 0, 1
TAB_RANK1, TAB_GATE1 = 0, 1


def _peer_route_kernel(x_ref, g_ref, wqt_ref, keys_ref, xnt_ref, tab0_ref, tab1_ref, cand_ref):
    xn = _rms(x_ref[...], g_ref[...])
    xnt = xn.T.astype(BF16)
    xnt_ref[...] = xnt
    qt = _dot(wqt_ref[...], xnt)
    half = keys_ref.shape[2]
    cand_ref[...] = jnp.full(cand_ref.shape, NEG_INF, F32)
    for h in range(P_HEADS):
        s0 = _dot(keys_ref[2 * h], qt[2 * h * half:(2 * h + 1) * half, :])
        s1 = _dot(keys_ref[2 * h + 1], qt[(2 * h + 1) * half:(2 * h + 2) * half, :])
        top0, rank0 = _top_ranked(s0, P_TOPK)
        top1, rank1 = _top_ranked(s1, P_TOPK)
        for r, (a, b) in enumerate(_CAND_PAIRS):
            cand_ref[r:r + 1, :] = top0[a] + top1[b]
        best, cand_rank = _top_ranked(cand_ref[...], P_TOPK)
        z = jnp.zeros_like(best[0])
        for c in best:
            z = z + jnp.exp(c - best[0])
        taken = jnp.where(cand_rank < P_TOPK, 1.0, 0.0)
        count0 = jnp.zeros_like(s0)
        for a in range(P_TOPK):
            rows = [r for r, (ca, _) in enumerate(_CAND_PAIRS) if ca == a]
            n_sel = jnp.sum(taken[rows[0]:rows[-1] + 1], axis=0, keepdims=True)
            count0 = jnp.where(rank0 == a, n_sel, count0)
        tab0_ref[2 * h + TAB_COUNT0] = count0
        tab0_ref[2 * h + TAB_GATE0] = jnp.exp(s0 - top0[0]) / z
        tab1_ref[2 * h + TAB_RANK1] = rank1.astype(BF16)
        tab1_ref[2 * h + TAB_GATE1] = jnp.exp(s1 - top1[0]).astype(BF16)


def _peer_route(x, g, wqt, keys, *, tt):
    t, d = x.shape
    nq = wqt.shape[0]
    hp, nk, half = keys.shape
    tab_spec = pl.BlockSpec((2 * P_HEADS, nk, tt), lambda i: (0, 0, i))
    return pl.pallas_call(
        _peer_route_kernel,
        grid=(t // tt,),
        in_specs=[pl.BlockSpec((tt, d), lambda i: (i, 0)),
                  pl.BlockSpec((1, d), lambda i: (0, 0)),
                  pl.BlockSpec((nq, d), lambda i: (0, 0)),
                  pl.BlockSpec((hp, nk, half), lambda i: (0, 0, 0))],
        out_specs=(pl.BlockSpec((d, tt), lambda i: (0, i)), tab_spec, tab_spec),
        out_shape=(jax.ShapeDtypeStruct((d, t), BF16),
                   jax.ShapeDtypeStruct((2 * P_HEADS, nk, t), F32),
                   jax.ShapeDtypeStruct((2 * P_HEADS, nk, t), BF16)),
        scratch_shapes=[pltpu.VMEM((_CAND_ROWS, tt), F32)],
        compiler_params=_params("parallel"),
        name="peer_route",
    )(x, g.reshape(1, d), wqt, keys)


def _gelu(x):
    return 0.5 * x * (1.0 + lax.erf(x * math.sqrt(0.5)))


def _rows_bf16(row, n):
    packed = jnp.broadcast_to(row, (BF16_SUBLANES, LANES)).astype(BF16)
    return jnp.concatenate([packed] * (n // BF16_SUBLANES), axis=0)


def _peer_expert_kernel(xnt_ref, tab0_ref, tab1_ref, u_ref, vt_ref, yt_ref, *, ne, ns):
    et = pl.program_id(1)
    tt = xnt_ref.shape[1]
    n_i = ne // N_KEYS
    assert n_i == SUBLANES and ns % N_KEYS == 0
    rows8 = pl.ds(pl.multiple_of(et * n_i, SUBLANES), SUBLANES)
    n_sub = ne // ns

    @pl.when(et == 0)
    def _():
        yt_ref[...] = jnp.zeros_like(yt_ref)

    def hidden(k):
        return _dot(u_ref[k * ns:(k + 1) * ns, :], xnt_ref[...])

    def gated(k, hid):
        first_rows = range(k * ns // N_KEYS, (k + 1) * ns // N_KEYS)
        nj = N_KEYS // 2
        blocks = {}
        for c in range(tt // LANES):
            cs = slice(c * LANES, (c + 1) * LANES)
            for jb in range(N_KEYS // nj):
                js = slice(jb * nj, (jb + 1) * nj)
                w = {r: jnp.zeros((nj, LANES), BF16) for r in first_rows}
                for h in range(P_HEADS):
                    rank1 = tab1_ref[2 * h + TAB_RANK1, js, cs]
                    gate1 = tab1_ref[2 * h + TAB_GATE1, js, cs]
                    count0 = tab0_ref[2 * h + TAB_COUNT0, rows8, cs]
                    gate0 = tab0_ref[2 * h + TAB_GATE0, rows8, cs]
                    for r in first_rows:
                        sel = rank1 < _rows_bf16(count0[r:r + 1], nj)
                        w[r] = w[r] + jnp.where(sel, _rows_bf16(gate0[r:r + 1], nj) * gate1, 0.0)
                for r in first_rows:
                    lr = r * N_KEYS - k * ns + jb * nj
                    blocks[r, jb, c] = _gelu(hid[lr:lr + nj, cs]).astype(BF16) * w[r]
        return jnp.concatenate(
            [jnp.concatenate([blocks[r, jb, c] for c in range(tt // LANES)], axis=1)
             for r in first_rows for jb in range(N_KEYS // nj)], axis=0)

    def accumulate(k, a):
        yt_ref[...] += _dot(vt_ref[:, k * ns:(k + 1) * ns], a)

    hid = hidden(0)
    a_prev = None
    for k in range(n_sub):
        hid_next = hidden(k + 1) if k + 1 < n_sub else None
        if a_prev is not None:
            accumulate(k - 1, a_prev)
        a_prev = gated(k, hid)
        hid = hid_next
    accumulate(n_sub - 1, a_prev)


def _peer_experts(xnt, tab0, tab1, u, vt, *, tt, ne, ns):
    d, t = xnt.shape
    n_tab, nk, _ = tab0.shape
    n_exp = u.shape[0]
    tab_spec = pl.BlockSpec((n_tab, nk, tt), lambda i, e: (0, 0, i))
    return pl.pallas_call(
        functools.partial(_peer_expert_kernel, ne=ne, ns=ns),
        grid=(t // tt, n_exp // ne),
        in_specs=[pl.BlockSpec((d, tt), lambda i, e: (0, i)),
                  tab_spec, tab_spec,
                  pl.BlockSpec((ne, d), lambda i, e: (e, 0)),
                  pl.BlockSpec((d, ne), lambda i, e: (0, e))],
        out_specs=pl.BlockSpec((d, tt), lambda i, e: (0, i)),
        out_shape=jax.ShapeDtypeStruct((d, t), F32),
        compiler_params=_params("parallel", "arbitrary"),
        name="peer_experts",
    )(xnt, tab0, tab1, u, vt)


def _add_transposed_kernel(x_ref, yt_ref, o_ref):
    o_ref[...] = x_ref[...] + yt_ref[...].T


def _add_transposed(x, yt, *, tt):
    t, d = x.shape
    return pl.pallas_call(
        _add_transposed_kernel,
        grid=(t // tt,),
        in_specs=[pl.BlockSpec((tt, d), lambda i: (i, 0)),
                  pl.BlockSpec((d, tt), lambda i: (0, i))],
        out_specs=pl.BlockSpec((tt, d), lambda i: (i, 0)),
        out_shape=jax.ShapeDtypeStruct((t, d), F32),
        compiler_params=_params("parallel"),
        name="add_transposed",
    )(x, yt)


def _peer(x, g, w_q, keys, u, v, *, tt_route, tt, ne):
    hp = keys.shape[0] * keys.shape[1]
    xnt, tab0, tab1 = _peer_route(x, g, w_q.T.astype(BF16),
                                  keys.reshape(hp, keys.shape[2], keys.shape[3]), tt=tt_route)
    yt = _peer_experts(xnt, tab0, tab1, u.astype(BF16), v.T.astype(BF16), tt=tt, ne=ne, ns=2 * N_KEYS)
    return _add_transposed(x, yt, tt=tt_route)


def _pick_tile(n, pref):
    t = pref
    while n % t:
        t //= 2
    return t


def kernel(x_prompt, x_sample, state_c, state_n, state_m, cache_k, cache_v, norm_mix, w_in_a, b_gate_a,
           norm_h_a, w_out_a, w_in_b, g_q_b, g_k_b, sink_b, w_out_b, norm_ffn, peer_wq, peer_keys,
           peer_u, peer_v):
    bp, sp, d = x_prompt.shape
    bs, ss, _ = x_sample.shape
    tp, ts = bp * sp, bs * ss
    t = tp + ts
    x = jnp.concatenate([x_prompt.reshape(tp, d), x_sample.reshape(ts, d)], axis=0)
    tm = _pick_tile(math.gcd(tp, ts), 512)
    tr = min(tm, 256)
    n_qkvo = 2 * A_HEADS * (A_DK + A_DV)
    chunk_p = _pick_tile(sp, 256)

    w_in = w_in_a[0]
    z = _norm_matmul(x, norm_mix[0], w_in[:, :n_qkvo].astype(BF16), tm=tm, tn=1024)
    w_gate = jnp.pad(w_in[:, n_qkvo:], ((0, 0), (0, LANES - 2 * A_HEADS)))
    zg = _norm_matmul(x, norm_mix[0], w_gate, tm=tm, tn=LANES, precise=True)
    h, c_p, n_p, m_p = _mlstm(z, zg, b_gate_a[0], norm_h_a[0], row0=0, n_seq=bp, seq_len=sp,
                              chunk=chunk_p, n_sub=1, t_total=t)
    h, c_s, n_s, m_s = _mlstm(z, zg, b_gate_a[0], norm_h_a[0], row0=tp, n_seq=bs, seq_len=ss,
                              chunk=ss, n_sub=4, t_total=t,
                              state=(state_c[0], state_n[0], state_m[0]), h_prev=h)
    x = _matmul_res(h, w_out_a[0].astype(BF16), x, tm=tm, tn=1024)
    x = _peer(x, norm_ffn[0], peer_wq[0], peer_keys[0], peer_u[0], peer_v[0], tt_route=tr, tt=tm, ne=1024)

    z = _norm_matmul(x, norm_mix[1], w_in_b[0].astype(BF16), tm=tm, tn=1024)
    o, kn_p = _swa_prompt(z, g_q_b[0], g_k_b[0], sink_b[0], n_seq=bp, seq_len=sp, t_total=t)
    o, ck_s, cv_s = _swa_sample(z, cache_k[0], cache_v[0], g_q_b[0], g_k_b[0], sink_b[0], o,
                                row0=tp, n_new=ss, n_batch=8)
    x = _matmul_res(o, w_out_b[0].astype(BF16), x, tm=tm, tn=1024)
    x = _peer(x, norm_ffn[1], peer_wq[1], peer_keys[1], peer_u[1], peer_v[1], tt_route=tr, tt=tm, ne=1024)

    bq = B_HEADS * B_HD
    bkv = B_KV_HEADS * B_HD
    v_p = z[:tp, bq + bkv:].reshape(bp, sp, bkv)[:, sp - WINDOW:]
    cache_shape = (1, bp, WINDOW, B_KV_HEADS, B_HD)
    return (x[:tp].reshape(bp, sp, d), x[tp:].reshape(bs, ss, d),
            c_p[None], n_p[None], m_p[None],
            kn_p.reshape(cache_shape), v_p.reshape(cache_shape),
            c_s[None], n_s[None], m_s[None], ck_s[None], cv_s[None])
```

```python
import functools
import math

import jax
import jax.numpy as jnp
from jax import lax
from jax.experimental import pallas as pl
from jax.experimental.pallas import tpu as pltpu

F32 = jnp.float32
BF16 = jnp.bfloat16
EPS = 1e-6
NEG_INF = float("-inf")

A_HEADS = 8
A_DK = 128
A_DV = 256
GATE_SOFTCAP = 15.0
B_HEADS = 32
B_KV_HEADS = 8
B_HD = 64
B_GROUP = B_HEADS // B_KV_HEADS
WINDOW = 128
P_HEADS = 8
N_KEYS = 128
P_TOPK = 16

LANES = 128
SUBLANES = 8
BF16_SUBLANES = 16
VMEM_LIMIT = 58 * 1024 * 1024


def _params(*sem):
    return pltpu.CompilerParams(dimension_semantics=sem, vmem_limit_bytes=VMEM_LIMIT)


def _dot(a, b):
    return jnp.dot(a, b, preferred_element_type=F32)


def _dot_nt(a, b):
    return lax.dot_general(a, b, (((1,), (1,)), ((), ())), preferred_element_type=F32)


def _dot_tn(a, b):
    return lax.dot_general(a, b, (((0,), (0,)), ((), ())), preferred_element_type=F32)


def _rms(x, g):
    return x * lax.rsqrt(jnp.mean(x * x, axis=-1, keepdims=True) + EPS) * g


def _norm_matmul_kernel(x_ref, g_ref, w_ref, o_ref, xn_ref, *, precise):
    @pl.when(pl.program_id(1) == 0)
    def _():
        xn_ref[...] = _rms(x_ref[...], g_ref[...]).astype(xn_ref.dtype)

    if precise:
        o_ref[...] = jnp.dot(xn_ref[...], w_ref[...], preferred_element_type=F32,
                             precision=lax.Precision.HIGHEST)
    else:
        o_ref[...] = _dot(xn_ref[...], w_ref[...])


def _norm_matmul(x, g, w, *, tm, tn, precise=False):
    t, d = x.shape
    n = w.shape[1]
    return pl.pallas_call(
        functools.partial(_norm_matmul_kernel, precise=precise),
        grid=(t // tm, n // tn),
        in_specs=[pl.BlockSpec((tm, d), lambda i, j: (i, 0)),
                  pl.BlockSpec((1, d), lambda i, j: (0, 0)),
                  pl.BlockSpec((d, tn), lambda i, j: (0, j))],
        out_specs=pl.BlockSpec((tm, tn), lambda i, j: (i, j)),
        out_shape=jax.ShapeDtypeStruct((t, n), F32),
        scratch_shapes=[pltpu.VMEM((tm, d), F32 if precise else BF16)],
        compiler_params=_params("parallel", "arbitrary"),
        name="norm_matmul_f32" if precise else "norm_matmul",
    )(x, g.reshape(1, d), w)


def _matmul_res_kernel(a_ref, w_ref, r_ref, o_ref):
    o_ref[...] = r_ref[...] + _dot(a_ref[...], w_ref[...])


def _matmul_res(a, w, r, *, tm, tn):
    t, k = a.shape
    n = w.shape[1]
    return pl.pallas_call(
        _matmul_res_kernel,
        grid=(t // tm, n // tn),
        in_specs=[pl.BlockSpec((tm, k), lambda i, j: (i, 0)),
                  pl.BlockSpec((k, tn), lambda i, j: (0, j)),
                  pl.BlockSpec((tm, tn), lambda i, j: (i, j))],
        out_specs=pl.BlockSpec((tm, tn), lambda i, j: (i, j)),
        out_shape=jax.ShapeDtypeStruct((t, n), F32),
        compiler_params=_params("parallel", "arbitrary"),
        name="matmul_res",
    )(a, w, r)


def _log_sigmoid(x):
    return jnp.minimum(x, 0.0) - jnp.log(1.0 + jnp.exp(-jnp.abs(x)))


def _mx(x):
    return x.astype(BF16) if x.shape[0] % 16 == 0 else x


def _mlstm_kernel(*refs, chunk, n_sub, has_state):
    if has_state:
        (q_ref, k_ref, v_ref, o_ref, gc_ref, gr_ref, bc_ref, br_ref, gh_ref,
         c0_ref, n0_ref, m0_ref, _, h_ref, c_ref, n_ref, m_ref) = refs
    else:
        (q_ref, k_ref, v_ref, o_ref, gc_ref, gr_ref, bc_ref, br_ref, gh_ref,
         h_ref, c_ref, n_ref, m_ref) = refs
    L = chunk

    @pl.when(pl.program_id(1) == 0)
    def _():
        if has_state:
            c_ref[...] = c0_ref[...]
            n_ref[...] = n0_ref[...]
            m_ref[...] = m0_ref[...]
        else:
            c_ref[...] = jnp.zeros_like(c_ref)
            n_ref[...] = jnp.zeros_like(n_ref)
            m_ref[...] = jnp.zeros_like(m_ref)

    t_idx = lax.broadcasted_iota(jnp.int32, (L, L), 0)
    s_idx = lax.broadcasted_iota(jnp.int32, (L, L), 1)
    causal = s_idx <= t_idx
    heads_out = [[None] * n_sub for _ in range(A_HEADS)]
    units = [(sub, h) for sub in range(n_sub) for h in range(A_HEADS)]
    stage = []

    for sub, h in units:
        r0, r1 = sub * L, (sub + 1) * L
        if h == 0:
            gcol = gc_ref[sub] + bc_ref[...]
            gcol = GATE_SOFTCAP * jnp.tanh(gcol / GATE_SOFTCAP)
            grow = gr_ref[sub] + br_ref[...]
            grow = GATE_SOFTCAP * jnp.tanh(grow / GATE_SOFTCAP)
            fcol_all = _log_sigmoid(gcol)
            frow_all = _log_sigmoid(grow)
        q = q_ref[r0:r1, h * A_DK:(h + 1) * A_DK]
        k = k_ref[r0:r1, h * A_DK:(h + 1) * A_DK] * (A_DK ** -0.5)
        v = v_ref[r0:r1, h * A_DV:(h + 1) * A_DV]
        i_col = gcol[:, h:h + 1]
        f_col = fcol_all[:, A_HEADS + h:A_HEADS + h + 1]
        i_row = grow[h:h + 1, :]
        f_row = frow_all[A_HEADS + h:A_HEADS + h + 1, :]
        c_prev = c_ref[sub, h]
        n_prev = n_ref[sub, h:h + 1, :]
        m_prev = m_ref[sub, h:h + 1, 0:1]

        b_col = jnp.sum(jnp.where(causal, f_row, 0.0), axis=1, keepdims=True)
        b_row = jnp.sum(jnp.where(t_idx <= s_idx, f_col, 0.0), axis=0, keepdims=True)

        log_intra = jnp.where(causal, b_col - b_row + i_row, NEG_INF)
        log_inter = b_col + m_prev
        m_t = jnp.maximum(log_inter, jnp.max(log_intra, axis=1, keepdims=True))
        w_intra = jnp.exp(log_intra - m_t)
        w_inter = jnp.exp(log_inter - m_t)

        b_end = b_col[L - 1:L, :]
        log_w_row = b_end - b_row + i_row
        log_w_col = b_end - b_col + i_col
        m_new = jnp.maximum(b_end + m_prev, jnp.max(log_w_row, axis=1, keepdims=True))
        decay = jnp.exp(b_end + m_prev - m_new)
        kw = k * jnp.exp(log_w_col - m_new)

        qb = _mx(q)
        vb = _mx(v)
        stage.append(dict(
            s=_dot_nt(qb, _mx(k)) * w_intra,
            qc=_dot(qb, c_prev.astype(qb.dtype)),
            kv=_dot_tn(_mx(kw), vb),
            qn=jnp.sum(q * n_prev, axis=1, keepdims=True),
            kw_sum=jnp.sum(kw, axis=0, keepdims=True),
            vb=vb, w_inter=w_inter, m_t=m_t, m_new=m_new, decay=decay, c_prev=c_prev, n_prev=n_prev))

    for st in stage:
        st["sv"] = _dot(_mx(st["s"]), st["vb"])

    for (sub, h), st in zip(units, stage):
        r0, r1 = sub * L, (sub + 1) * L
        num = st["w_inter"] * st["qc"] + st["sv"]
        den = st["w_inter"] * st["qn"] + jnp.sum(st["s"], axis=1, keepdims=True)
        hh = num / jnp.maximum(jnp.abs(den), jnp.exp(-st["m_t"]))
        c_ref[sub, h] = st["decay"] * st["c_prev"] + st["kv"]
        n_ref[sub, h:h + 1, :] = st["decay"] * st["n_prev"] + st["kw_sum"]
        m_ref[sub, h:h + 1, :] = jnp.broadcast_to(st["m_new"], (1, LANES))

        hn = hh * lax.rsqrt(jnp.mean(hh * hh, axis=-1, keepdims=True) + EPS)
        hn = hn * gh_ref[:, h * A_DV:(h + 1) * A_DV]
        heads_out[h][sub] = hn * jax.nn.sigmoid(o_ref[r0:r1, h * A_DV:(h + 1) * A_DV])

    for h in range(A_HEADS):
        rows = heads_out[h][0] if n_sub == 1 else jnp.concatenate(heads_out[h], axis=0)
        h_ref[:, h * A_DV:(h + 1) * A_DV] = rows.astype(h_ref.dtype)


def _mlstm(z, zg, b_gate, g_h, *, row0, n_seq, seq_len, chunk, n_sub, t_total, state=None, h_prev=None):
    L = chunk
    nc = seq_len // L
    assert n_sub == 1 or nc == 1
    hd = A_HEADS * A_DK
    hv = A_HEADS * A_DV
    R = n_sub * L
    rb0 = row0 // R
    rows = n_seq * seq_len
    gates = zg[row0:row0 + rows, :2 * A_HEADS].reshape(n_seq * nc, L, 2 * A_HEADS)
    gates_t = jnp.swapaxes(gates, 1, 2)
    has_state = state is not None

    def rmap(col):
        return lambda s, c: (rb0 + s * nc + c, col)

    in_specs = [pl.BlockSpec((R, hd), rmap(0)),
                pl.BlockSpec((R, hd), rmap(1)),
                pl.BlockSpec((R, hv), rmap(hd * 2 // hv)),
                pl.BlockSpec((R, hv), rmap(hd * 2 // hv + 1)),
                pl.BlockSpec((n_sub, L, 2 * A_HEADS), lambda s, c: (s * nc + c, 0, 0)),
                pl.BlockSpec((n_sub, 2 * A_HEADS, L), lambda s, c: (s * nc + c, 0, 0)),
                pl.BlockSpec((1, 2 * A_HEADS), lambda s, c: (0, 0)),
                pl.BlockSpec((2 * A_HEADS, 1), lambda s, c: (0, 0)),
                pl.BlockSpec((1, hv), lambda s, c: (0, 0))]
    args = [z, z, z, z, gates, gates_t, b_gate.reshape(1, -1), b_gate.reshape(-1, 1), g_h.reshape(1, hv)]
    state_specs = [pl.BlockSpec((n_sub, A_HEADS, A_DK, A_DV), lambda s, c: (s, 0, 0, 0)),
                   pl.BlockSpec((n_sub, A_HEADS, A_DK), lambda s, c: (s, 0, 0)),
                   pl.BlockSpec((n_sub, A_HEADS, LANES), lambda s, c: (s, 0, 0))]
    aliases = {}
    if has_state:
        c0, n0, m0 = state
        in_specs += state_specs
        args += [c0, n0, jnp.broadcast_to(m0[..., None], m0.shape + (LANES,))]
        in_specs.append(pl.BlockSpec(memory_space=pl.ANY))
        args.append(h_prev)
        aliases = {len(args) - 1: 0}
    out_shape = (jax.ShapeDtypeStruct((t_total, hv), BF16),
                 jax.ShapeDtypeStruct((n_seq, A_HEADS, A_DK, A_DV), F32),
                 jax.ShapeDtypeStruct((n_seq, A_HEADS, A_DK), F32),
                 jax.ShapeDtypeStruct((n_seq, A_HEADS, LANES), F32))
    h, c, n, m = pl.pallas_call(
        functools.partial(_mlstm_kernel, chunk=L, n_sub=n_sub, has_state=has_state),
        grid=(n_seq // n_sub, nc),
        in_specs=in_specs,
        out_specs=(pl.BlockSpec((R, hv), rmap(0)),) + tuple(state_specs),
        out_shape=out_shape,
        input_output_aliases=aliases,
        compiler_params=_params("parallel", "arbitrary"),
        name="mlstm_state" if has_state else "mlstm",
    )(*args)
    return h, c, n, m[..., 0]


def _alibi_slope(h):
    return 2.0 ** (-8.0 * (h + 1.0) / B_HEADS)


def _sink_softmax(logits, sink):
    m = jnp.maximum(jnp.max(logits, axis=-1, keepdims=True), sink)
    p = jnp.exp(logits - m)
    return p / (jnp.sum(p, axis=-1, keepdims=True) + jnp.exp(sink - m))


def _swa_prompt_kernel(sink_ref, q_ref, kp_ref, ko_ref, vp_ref, vo_ref, gq_ref, gk_ref,
                       o_ref, kn_ref):
    blk = pl.program_id(1)
    W = WINDOW
    t_idx = lax.broadcasted_iota(jnp.int32, (W, 2 * W), 0)
    s_idx = lax.broadcasted_iota(jnp.int32, (W, 2 * W), 1)
    dist = W + t_idx - s_idx
    valid = (dist >= 0) & (dist <= W) & ((s_idx >= W) | (blk > 0))
    dist_f = dist.astype(F32)
    gq = gq_ref[...]
    gk = gk_ref[...]
    qs, ks, vs, kns = [], [], [], []
    for kh in range(B_KV_HEADS):
        lo, hi = kh * B_HD, (kh + 1) * B_HD
        k_all = jnp.concatenate([kp_ref[:, lo:hi], ko_ref[:, lo:hi]], axis=0)
        v_all = jnp.concatenate([vp_ref[:, lo:hi], vo_ref[:, lo:hi]], axis=0)
        kn = _rms(k_all, gk)
        kns.append(kn[W:])
        ks.append(kn.astype(BF16))
        vs.append(v_all.astype(BF16))
        heads = range(kh * B_GROUP, (kh + 1) * B_GROUP)
        qs.append(jnp.concatenate([_rms(q_ref[:, h * B_HD:(h + 1) * B_HD], gq) for h in heads],
                                  axis=0).astype(BF16))
    scores = [_dot_nt(qs[kh], ks[kh]) for kh in range(B_KV_HEADS)]
    probs = []
    for kh in range(B_KV_HEADS):
        pieces = []
        for g in range(B_GROUP):
            h = kh * B_GROUP + g
            s = scores[kh][g * W:(g + 1) * W] * (B_HD ** -0.5)
            logits = jnp.where(valid, s - _alibi_slope(h) * dist_f, NEG_INF)
            pieces.append(_sink_softmax(logits, sink_ref[h]).astype(BF16))
        probs.append(jnp.concatenate(pieces, axis=0))
    outs = [_dot(probs[kh], vs[kh]) for kh in range(B_KV_HEADS)]
    o_ref[...] = jnp.concatenate(
        [outs[kh][g * W:(g + 1) * W] for kh in range(B_KV_HEADS) for g in range(B_GROUP)],
        axis=-1).astype(o_ref.dtype)
    kn_ref[0] = jnp.concatenate(kns, axis=-1)


def _swa_prompt(z, g_q, g_k, sinks, *, n_seq, seq_len, t_total):
    W = WINDOW
    nb = seq_len // W
    bq = B_HEADS * B_HD
    bkv = B_KV_HEADS * B_HD
    kcol = bq // bkv

    def own(col):
        return lambda b, i: (b * nb + i, col)

    def prev(col):
        return lambda b, i: (b * nb + jnp.maximum(i - 1, 0), col)

    return pl.pallas_call(
        _swa_prompt_kernel,
        grid=(n_seq, nb),
        in_specs=[pl.BlockSpec(memory_space=pltpu.SMEM),
                  pl.BlockSpec((W, bq), own(0)),
                  pl.BlockSpec((W, bkv), prev(kcol)),
                  pl.BlockSpec((W, bkv), own(kcol)),
                  pl.BlockSpec((W, bkv), prev(kcol + 1)),
                  pl.BlockSpec((W, bkv), own(kcol + 1)),
                  pl.BlockSpec((1, B_HD), lambda b, i: (0, 0)),
                  pl.BlockSpec((1, B_HD), lambda b, i: (0, 0))],
        out_specs=(pl.BlockSpec((W, bq), own(0)),
                   pl.BlockSpec((1, W, bkv), lambda b, i: (b, 0, 0))),
        out_shape=(jax.ShapeDtypeStruct((t_total, bq), BF16),
                   jax.ShapeDtypeStruct((n_seq, W, bkv), F32)),
        compiler_params=_params("parallel", "arbitrary"),
        name="swa_prompt",
    )(sinks, z, z, z, z, z, g_q.reshape(1, B_HD), g_k.reshape(1, B_HD))


def _swa_sample_kernel(sink_ref, q_ref, k_ref, v_ref, ck_ref, cv_ref, gq_ref, gk_ref, _,
                       o_ref, cko_ref, cvo_ref, *, n_new, n_batch):
    W = WINDOW
    T = n_new
    R = B_HEADS * T
    GT = B_GROUP * T
    r_col = lax.broadcasted_iota(jnp.int32, (R, 1), 0)
    s_idx = lax.broadcasted_iota(jnp.int32, (R, W + T), 1)
    t_col = r_col
    slope_col = jnp.full((R, 1), _alibi_slope(0), F32)
    sink_col = jnp.full((R, 1), sink_ref[0], F32)
    for h in range(1, B_HEADS):
        in_later_head = r_col >= h * T
        t_col = jnp.where(in_later_head, r_col - h * T, t_col)
        slope_col = jnp.where(in_later_head, _alibi_slope(h), slope_col)
        sink_col = jnp.where(in_later_head, sink_ref[h], sink_col)
    dist = W + t_col - s_idx
    valid = (dist >= 0) & (dist <= W)
    bias = slope_col * dist.astype(F32)
    gq = gq_ref[...]
    gk = gk_ref[...]
    qs, ks, vs = [], [], []
    for bi in range(n_batch):
        r0, r1 = bi * T, (bi + 1) * T
        for kh in range(B_KV_HEADS):
            lo, hi = kh * B_HD, (kh + 1) * B_HD
            kn_new = _rms(k_ref[r0:r1, lo:hi], gk)
            k_all = jnp.concatenate([ck_ref[bi, :, lo:hi], kn_new], axis=0)
            v_all = jnp.concatenate([cv_ref[bi, :, lo:hi], v_ref[r0:r1, lo:hi]], axis=0)
            cko_ref[bi, :, lo:hi] = k_all[T:]
            cvo_ref[bi, :, lo:hi] = v_all[T:]
            ks.append(k_all)
            vs.append(v_all)
            heads = range(kh * B_GROUP, (kh + 1) * B_GROUP)
            qs.append(jnp.concatenate(
                [_rms(q_ref[r0:r1, h * B_HD:(h + 1) * B_HD], gq) for h in heads], axis=0))
    scores = [_dot_nt(q, k) for q, k in zip(qs, ks)]
    probs = []
    for bi in range(n_batch):
        s = jnp.concatenate(scores[bi * B_KV_HEADS:(bi + 1) * B_KV_HEADS], axis=0)
        logits = jnp.where(valid, s * (B_HD ** -0.5) - bias, NEG_INF)
        probs.append(_sink_softmax(logits, sink_col))
    outs = [_dot(probs[i // B_KV_HEADS][(i % B_KV_HEADS) * GT:(i % B_KV_HEADS + 1) * GT], v)
            for i, v in enumerate(vs)]
    rows_out = []
    for bi in range(n_batch):
        rows_out.append(jnp.concatenate(
            [outs[bi * B_KV_HEADS + kh][g * T:(g + 1) * T]
             for kh in range(B_KV_HEADS) for g in range(B_GROUP)], axis=-1))
    o_ref[...] = jnp.concatenate(rows_out, axis=0).astype(o_ref.dtype)


def _swa_sample(z, cache_k, cache_v, g_q, g_k, sinks, o_prev, *, row0, n_new, n_batch):
    W = WINDOW
    n_seq = cache_k.shape[0]
    bq = B_HEADS * B_HD
    bkv = B_KV_HEADS * B_HD
    kcol = bq // bkv
    rows = n_new * n_batch
    rb0 = row0 // rows
    ck = cache_k.reshape(n_seq, W, bkv)
    cv = cache_v.reshape(n_seq, W, bkv)
    cache_spec = pl.BlockSpec((n_batch, W, bkv), lambda i: (i, 0, 0))
    o, cko, cvo = pl.pallas_call(
        functools.partial(_swa_sample_kernel, n_new=n_new, n_batch=n_batch),
        grid=(n_seq // n_batch,),
        in_specs=[pl.BlockSpec(memory_space=pltpu.SMEM),
                  pl.BlockSpec((rows, bq), lambda i: (rb0 + i, 0)),
                  pl.BlockSpec((rows, bkv), lambda i: (rb0 + i, kcol)),
                  pl.BlockSpec((rows, bkv), lambda i: (rb0 + i, kcol + 1)),
                  cache_spec, cache_spec,
                  pl.BlockSpec((1, B_HD), lambda i: (0, 0)),
                  pl.BlockSpec((1, B_HD), lambda i: (0, 0)),
                  pl.BlockSpec(memory_space=pl.ANY)],
        out_specs=(pl.BlockSpec((rows, bq), lambda i: (rb0 + i, 0)), cache_spec, cache_spec),
        out_shape=(jax.ShapeDtypeStruct(o_prev.shape, o_prev.dtype),
                   jax.ShapeDtypeStruct(ck.shape, cache_k.dtype),
                   jax.ShapeDtypeStruct(cv.shape, cache_v.dtype)),
        input_output_aliases={8: 0},
        compiler_params=_params("parallel"),
        name="swa_sample",
    )(sinks, z, z, z, ck, cv, g_q.reshape(1, B_HD), g_k.reshape(1, B_HD), o_prev)
    return o, cko.reshape(cache_k.shape), cvo.reshape(cache_v.shape)


def _top_ranked(s, k):
    rows = s.shape[0]
    ridx = lax.broadcasted_iota(jnp.int32, s.shape, 0).astype(F32)
    rank = jnp.full(s.shape, float(k), F32)
    vals = []
    for r in range(k):
        m = jnp.max(s, axis=0, keepdims=True)
        vals.append(m)
        first = ridx == jnp.min(jnp.where(s == m, ridx, float(rows)), axis=0, keepdims=True)
        rank = jnp.where(first, float(r), rank)
        s = jnp.where(first, NEG_INF, s)
    return vals, rank


def _top_ranked_distinct(s, k):
    rank = jnp.full(s.shape, float(k), F32)
    vals = []
    for r in range(k):
        m = jnp.max(s, axis=0, keepdims=True)
        vals.append(m)
        hit = s == m
        rank = jnp.where(hit, float(r), rank)
        s = jnp.where(hit, NEG_INF, s)
    return vals, rank


def _ranked_ok(rank, k):
    n_ranked = jnp.sum(jnp.where(rank < k, 1.0, 0.0), axis=0, keepdims=True)
    return jnp.where(n_ranked == k, 1.0, 0.0)


_CAND_PAIRS = [(a, b) for a in range(P_TOPK) for b in range(P_TOPK) if (a + 1) * (b + 1) <= P_TOPK]
_CAND_ROWS = -(-len(_CAND_PAIRS) // 8) * 8


TAB_COUNT0, TAB_GATE0 = 0, 1
TAB_RANK1, TAB_GATE1 = 0, 1


def _peer_route_kernel(x_ref, g_ref, wqt_ref, keys_ref, xnt_ref, tab0_ref, tab1_ref,
                       qt_ref, cand_ref, tied_ref):
    xn = _rms(x_ref[...], g_ref[...])
    xnt = xn.T.astype(BF16)
    xnt_ref[...] = xnt
    qt_ref[...] = _dot(wqt_ref[...], xnt)
    half = keys_ref.shape[2]
    cand_ref[...] = jnp.full(cand_ref.shape, NEG_INF, F32)

    def route_head(h, top_fn):
        row0 = 2 * h * half if isinstance(h, int) else pl.multiple_of(2 * h * half, half)
        q0 = qt_ref[pl.ds(row0, half), :]
        q1 = qt_ref[pl.ds(row0 + half, half), :]
        s0 = _dot(keys_ref[2 * h], q0)
        s1 = _dot(keys_ref[2 * h + 1], q1)
        top0, rank0 = top_fn(s0, P_TOPK)
        top1, rank1 = top_fn(s1, P_TOPK)
        for r, (a, b) in enumerate(_CAND_PAIRS):
            cand_ref[r:r + 1, :] = top0[a] + top1[b]
        best, cand_rank = top_fn(cand_ref[...], P_TOPK)
        z = jnp.zeros_like(best[0])
        for c in best:
            z = z + jnp.exp(c - best[0])
        taken = jnp.where(cand_rank < P_TOPK, 1.0, 0.0)
        count0 = jnp.zeros_like(s0)
        for a in range(P_TOPK):
            rows = [r for r, (ca, _) in enumerate(_CAND_PAIRS) if ca == a]
            n_sel = jnp.sum(taken[rows[0]:rows[-1] + 1], axis=0, keepdims=True)
            count0 = jnp.where(rank0 == a, n_sel, count0)
        tab0_ref[2 * h + TAB_COUNT0] = 2.0 * count0 - 1.0
        tab0_ref[2 * h + TAB_GATE0] = jnp.exp(s0 - top0[0]) / z
        tab1_ref[2 * h + TAB_RANK1] = (2.0 * rank1).astype(BF16)
        tab1_ref[2 * h + TAB_GATE1] = jnp.exp(s1 - top1[0]).astype(BF16)
        return _ranked_ok(rank0, P_TOPK) * _ranked_ok(rank1, P_TOPK) * _ranked_ok(cand_rank, P_TOPK)

    for h in range(P_HEADS):
        ok = route_head(h, _top_ranked_distinct)
        tied_ref[h] = (jnp.min(ok) < 1.0).astype(jnp.int32)

    def redo(h, carry):
        @pl.when(tied_ref[h] != 0)
        def _():
            route_head(h, _top_ranked)
        return carry

    lax.fori_loop(0, P_HEADS, redo, 0)


def _peer_route(x, g, wqt, keys, *, tt):
    t, d = x.shape
    nq = wqt.shape[0]
    hp, nk, half = keys.shape
    tab_spec = pl.BlockSpec((2 * P_HEADS, nk, tt), lambda i: (0, 0, i))
    return pl.pallas_call(
        _peer_route_kernel,
        grid=(t // tt,),
        in_specs=[pl.BlockSpec((tt, d), lambda i: (i, 0)),
                  pl.BlockSpec((1, d), lambda i: (0, 0)),
                  pl.BlockSpec((nq, d), lambda i: (0, 0)),
                  pl.BlockSpec((hp, nk, half), lambda i: (0, 0, 0))],
        out_specs=(pl.BlockSpec((d, tt), lambda i: (0, i)), tab_spec, tab_spec),
        out_shape=(jax.ShapeDtypeStruct((d, t), BF16),
                   jax.ShapeDtypeStruct((2 * P_HEADS, nk, t), F32),
                   jax.ShapeDtypeStruct((2 * P_HEADS, nk, t), BF16)),
        scratch_shapes=[pltpu.VMEM((nq, tt), F32), pltpu.VMEM((_CAND_ROWS, tt), F32),
                        pltpu.SMEM((P_HEADS,), jnp.int32)],
        compiler_params=_params("parallel"),
        name="peer_route",
    )(x, g.reshape(1, d), wqt, keys)


def _gelu(x):
    return 0.5 * x * (1.0 + lax.erf(x * math.sqrt(0.5)))


def _rows_bf16(row, n):
    packed = jnp.broadcast_to(row, (BF16_SUBLANES, LANES)).astype(BF16)
    return jnp.concatenate([packed] * (n // BF16_SUBLANES), axis=0)


def _peer_expert_kernel(xnt_ref, tab0_ref, tab1_ref, u_ref, vt_ref, yt_ref, w0_ref, *, ne, nc):
    et = pl.program_id(1)
    n_et = pl.num_programs(1)
    tt = xnt_ref.shape[1]
    n_i = ne // N_KEYS
    assert n_i == SUBLANES and nc % LANES == 0
    n_chunk = tt // nc
    row_group = 4
    blocks_rc = [(r, c) for r in range(n_i) for c in range(nc // LANES)]

    def hidden(k):
        return _dot(u_ref[...], xnt_ref[:, k * nc:(k + 1) * nc])

    def gates(k, tile):
        rows8 = pl.ds(pl.multiple_of(tile * n_i, SUBLANES), SUBLANES)
        blocks = {}
        for c in range(nc // LANES):
            cs = slice(k * nc + c * LANES, k * nc + (c + 1) * LANES)
            for r0 in range(0, n_i, row_group):
                rows = range(r0, r0 + row_group)
                w = {r: jnp.zeros((N_KEYS, LANES), BF16) for r in rows}
                for h in range(P_HEADS):
                    rank1 = tab1_ref[2 * h + TAB_RANK1, :, cs]
                    gate1 = tab1_ref[2 * h + TAB_GATE1, :, cs]
                    count0 = tab0_ref[2 * h + TAB_COUNT0, rows8, cs]
                    gate0 = tab0_ref[2 * h + TAB_GATE0, rows8, cs]
                    for r in rows:
                        sel_gate1 = jnp.minimum(
                            jnp.maximum(_rows_bf16(count0[r:r + 1], N_KEYS) - rank1, 0.0), gate1)
                        w[r] = w[r] + sel_gate1 * _rows_bf16(gate0[r:r + 1], N_KEYS)
                for r in rows:
                    blocks[r, c] = w[r]
        return blocks

    def activate(hid, w):
        return jnp.concatenate(
            [jnp.concatenate(
                [_gelu(hid[r * N_KEYS:(r + 1) * N_KEYS, c * LANES:(c + 1) * LANES]).astype(BF16) * w[r, c]
                 for c in range(nc // LANES)], axis=1)
             for r in range(n_i)], axis=0)

    def accumulate(k, a):
        yt_ref[:, k * nc:(k + 1) * nc] += _dot(vt_ref[...], a)

    def store_gates(w):
        for r, c in blocks_rc:
            w0_ref[r * N_KEYS:(r + 1) * N_KEYS, c * LANES:(c + 1) * LANES] = w[r, c]

    @pl.when(et == 0)
    def _():
        yt_ref[...] = jnp.zeros_like(yt_ref)
        store_gates(gates(0, et))

    w = {(r, c): w0_ref[r * N_KEYS:(r + 1) * N_KEYS, c * LANES:(c + 1) * LANES] for r, c in blocks_rc}
    hid = hidden(0)
    a_prev = None
    for k in range(n_chunk):
        last = k + 1 == n_chunk
        hid_next = None if last else hidden(k + 1)
        if a_prev is not None:
            accumulate(k - 1, a_prev)
        w_next = None if last else gates(k + 1, et)
        a_prev = activate(hid, w)
        hid, w = hid_next, w_next
    w_ahead = gates(0, jnp.minimum(et + 1, n_et - 1))
    accumulate(n_chunk - 1, a_prev)
    store_gates(w_ahead)


def _peer_experts(xnt, tab0, tab1, u, vt, *, tt, ne, nc):
    d, t = xnt.shape
    n_tab, nk, _ = tab0.shape
    n_exp = u.shape[0]
    once = pl.Buffered(1)
    tab_spec = pl.BlockSpec((n_tab, nk, tt), lambda i, e: (0, 0, i), pipeline_mode=once)
    return pl.pallas_call(
        functools.partial(_peer_expert_kernel, ne=ne, nc=nc),
        grid=(t // tt, n_exp // ne),
        in_specs=[pl.BlockSpec((d, tt), lambda i, e: (0, i), pipeline_mode=once),
                  tab_spec, tab_spec,
                  pl.BlockSpec((ne, d), lambda i, e: (e, 0)),
                  pl.BlockSpec((d, ne), lambda i, e: (0, e))],
        out_specs=pl.BlockSpec((d, tt), lambda i, e: (0, i)),
        out_shape=jax.ShapeDtypeStruct((d, t), F32),
        scratch_shapes=[pltpu.VMEM((ne, nc), BF16)],
        compiler_params=_params("parallel", "arbitrary"),
        name="peer_experts",
    )(xnt, tab0, tab1, u, vt)


def _add_transposed_kernel(x_ref, yt_ref, o_ref):
    o_ref[...] = x_ref[...] + yt_ref[...].T


def _add_transposed(x, yt, *, tt):
    t, d = x.shape
    return pl.pallas_call(
        _add_transposed_kernel,
        grid=(t // tt,),
        in_specs=[pl.BlockSpec((tt, d), lambda i: (i, 0)),
                  pl.BlockSpec((d, tt), lambda i: (0, i))],
        out_specs=pl.BlockSpec((tt, d), lambda i: (i, 0)),
        out_shape=jax.ShapeDtypeStruct((t, d), F32),
        compiler_params=_params("parallel"),
        name="add_transposed",
    )(x, yt)


def _transpose_cast_kernel(x_ref, o_ref):
    o_ref[...] = x_ref[...].T.astype(o_ref.dtype)


def _transpose_cast(x, dtype, *, rows):
    n, d = x.shape
    return pl.pallas_call(
        _transpose_cast_kernel,
        grid=(n // rows,),
        in_specs=[pl.BlockSpec((rows, d), lambda i: (i, 0))],
        out_specs=pl.BlockSpec((d, rows), lambda i: (0, i)),
        out_shape=jax.ShapeDtypeStruct((d, n), dtype),
        compiler_params=_params("parallel"),
        name="transpose_cast",
    )(x)


def _peer(x, g, w_q, keys, u, v, *, tt_route, tt, ne):
    hp = keys.shape[0] * keys.shape[1]
    xnt, tab0, tab1 = _peer_route(x, g, w_q.T.astype(BF16),
                                  keys.reshape(hp, keys.shape[2], keys.shape[3]), tt=tt_route)
    yt = _peer_experts(xnt, tab0, tab1, u.astype(BF16), _transpose_cast(v, BF16, rows=512), tt=tt, ne=ne,
                       nc=min(tt, 2 * LANES))
    return _add_transposed(x, yt, tt=tt_route)


def _pick_tile(n, pref):
    t = pref
    while n % t:
        t //= 2
    return t


def kernel(x_prompt, x_sample, state_c, state_n, state_m, cache_k, cache_v, norm_mix, w_in_a, b_gate_a,
           norm_h_a, w_out_a, w_in_b, g_q_b, g_k_b, sink_b, w_out_b, norm_ffn, peer_wq, peer_keys,
           peer_u, peer_v):
    bp, sp, d = x_prompt.shape
    bs, ss, _ = x_sample.shape
    tp, ts = bp * sp, bs * ss
    t = tp + ts
    x = jnp.concatenate([x_prompt.reshape(tp, d), x_sample.reshape(ts, d)], axis=0)
    tm = _pick_tile(math.gcd(tp, ts), 512)
    tr = min(tm, 256)
    te = _pick_tile(t, 1024)
    n_qkvo = 2 * A_HEADS * (A_DK + A_DV)
    chunk_p = _pick_tile(sp, 256)

    w_in = w_in_a[0]
    z = _norm_matmul(x, norm_mix[0], w_in[:, :n_qkvo].astype(BF16), tm=tm, tn=1024)
    w_gate = jnp.pad(w_in[:, n_qkvo:], ((0, 0), (0, LANES - 2 * A_HEADS)))
    zg = _norm_matmul(x, norm_mix[0], w_gate, tm=tm, tn=LANES, precise=True)
    h, c_p, n_p, m_p = _mlstm(z, zg, b_gate_a[0], norm_h_a[0], row0=0, n_seq=bp, seq_len=sp,
                              chunk=chunk_p, n_sub=1, t_total=t)
    h, c_s, n_s, m_s = _mlstm(z, zg, b_gate_a[0], norm_h_a[0], row0=tp, n_seq=bs, seq_len=ss,
                              chunk=ss, n_sub=4, t_total=t,
                              state=(state_c[0], state_n[0], state_m[0]), h_prev=h)
    x = _matmul_res(h, w_out_a[0].astype(BF16), x, tm=tm, tn=1024)
    x = _peer(x, norm_ffn[0], peer_wq[0], peer_keys[0], peer_u[0], peer_v[0], tt_route=tr, tt=te, ne=1024)

    z = _norm_matmul(x, norm_mix[1], w_in_b[0].astype(BF16), tm=tm, tn=1024)
    o, kn_p = _swa_prompt(z, g_q_b[0], g_k_b[0], sink_b[0], n_seq=bp, seq_len=sp, t_total=t)
    o, ck_s, cv_s = _swa_sample(z, cache_k[0], cache_v[0], g_q_b[0], g_k_b[0], sink_b[0], o,
                                row0=tp, n_new=ss, n_batch=8)
    x = _matmul_res(o, w_out_b[0].astype(BF16), x, tm=tm, tn=1024)
    x = _peer(x, norm_ffn[1], peer_wq[1], peer_keys[1], peer_u[1], peer_v[1], tt_route=tr, tt=te, ne=1024)

    bq = B_HEADS * B_HD
    bkv = B_KV_HEADS * B_HD
    v_p = z[:tp, bq + bkv:].reshape(bp, sp, bkv)[:, sp - WINDOW:]
    cache_shape = (1, bp, WINDOW, B_KV_HEADS, B_HD)
    return (x[:tp].reshape(bp, sp, d), x[tp:].reshape(bs, ss, d),
            c_p[None], n_p[None], m_p[None],
            kn_p.reshape(cache_shape), v_p.reshape(cache_shape),
            c_s[None], n_s[None], m_s[None], ck_s[None], cv_s[None])
```

```python
import functools
import math

import jax
import jax.numpy as jnp
from jax import lax
from jax.experimental import pallas as pl
from jax.experimental.pallas import tpu as pltpu

F32 = jnp.float32
BF16 = jnp.bfloat16
EPS = 1e-6
NEG_INF = float("-inf")

A_HEADS = 8
A_DK = 128
A_DV = 256
GATE_SOFTCAP = 15.0
B_HEADS = 32
B_KV_HEADS = 8
B_HD = 64
B_GROUP = B_HEADS // B_KV_HEADS
WINDOW = 128
P_HEADS = 8
N_KEYS = 128
P_TOPK = 16

LANES = 128
SUBLANES = 8
BF16_SUBLANES = 16
VMEM_LIMIT = 58 * 1024 * 1024


def _params(*sem):
    return pltpu.CompilerParams(dimension_semantics=sem, vmem_limit_bytes=VMEM_LIMIT)


def _dot(a, b):
    return jnp.dot(a, b, preferred_element_type=F32)


def _dot_nt(a, b):
    return lax.dot_general(a, b, (((1,), (1,)), ((), ())), preferred_element_type=F32)


def _dot_tn(a, b):
    return lax.dot_general(a, b, (((0,), (0,)), ((), ())), preferred_element_type=F32)


def _rms(x, g):
    return x * lax.rsqrt(jnp.mean(x * x, axis=-1, keepdims=True) + EPS) * g


def _norm_matmul_kernel(x_ref, g_ref, w_ref, o_ref, xn_ref, *, precise):
    @pl.when(pl.program_id(1) == 0)
    def _():
        xn_ref[...] = _rms(x_ref[...], g_ref[...]).astype(xn_ref.dtype)

    if precise:
        o_ref[...] = jnp.dot(xn_ref[...], w_ref[...], preferred_element_type=F32,
                             precision=lax.Precision.HIGHEST)
    else:
        o_ref[...] = _dot(xn_ref[...], w_ref[...])


def _norm_matmul(x, g, w, *, tm, tn, precise=False):
    t, d = x.shape
    n = w.shape[1]
    return pl.pallas_call(
        functools.partial(_norm_matmul_kernel, precise=precise),
        grid=(t // tm, n // tn),
        in_specs=[pl.BlockSpec((tm, d), lambda i, j: (i, 0)),
                  pl.BlockSpec((1, d), lambda i, j: (0, 0)),
                  pl.BlockSpec((d, tn), lambda i, j: (0, j))],
        out_specs=pl.BlockSpec((tm, tn), lambda i, j: (i, j)),
        out_shape=jax.ShapeDtypeStruct((t, n), F32),
        scratch_shapes=[pltpu.VMEM((tm, d), F32 if precise else BF16)],
        compiler_params=_params("parallel", "arbitrary"),
        name="norm_matmul_f32" if precise else "norm_matmul",
    )(x, g.reshape(1, d), w)


def _matmul_res_kernel(a_ref, w_ref, r_ref, o_ref):
    o_ref[...] = r_ref[...] + _dot(a_ref[...], w_ref[...])


def _matmul_res(a, w, r, *, tm, tn):
    t, k = a.shape
    n = w.shape[1]
    return pl.pallas_call(
        _matmul_res_kernel,
        grid=(t // tm, n // tn),
        in_specs=[pl.BlockSpec((tm, k), lambda i, j: (i, 0)),
                  pl.BlockSpec((k, tn), lambda i, j: (0, j)),
                  pl.BlockSpec((tm, tn), lambda i, j: (i, j))],
        out_specs=pl.BlockSpec((tm, tn), lambda i, j: (i, j)),
        out_shape=jax.ShapeDtypeStruct((t, n), F32),
        compiler_params=_params("parallel", "arbitrary"),
        name="matmul_res",
    )(a, w, r)


def _log_sigmoid(x):
    return jnp.minimum(x, 0.0) - jnp.log(1.0 + jnp.exp(-jnp.abs(x)))


def _mx(x):
    return x.astype(BF16) if x.shape[0] % 16 == 0 else x


def _mlstm_kernel(*refs, chunk, n_sub, has_state):
    if has_state:
        (q_ref, k_ref, v_ref, o_ref, gc_ref, gr_ref, bc_ref, br_ref, gh_ref,
         c0_ref, n0_ref, m0_ref, _, h_ref, c_ref, n_ref, m_ref) = refs
    else:
        (q_ref, k_ref, v_ref, o_ref, gc_ref, gr_ref, bc_ref, br_ref, gh_ref,
         h_ref, c_ref, n_ref, m_ref) = refs
    L = chunk

    @pl.when(pl.program_id(1) == 0)
    def _():
        if has_state:
            c_ref[...] = c0_ref[...]
            n_ref[...] = n0_ref[...]
            m_ref[...] = m0_ref[...]
        else:
            c_ref[...] = jnp.zeros_like(c_ref)
            n_ref[...] = jnp.zeros_like(n_ref)
            m_ref[...] = jnp.zeros_like(m_ref)

    t_idx = lax.broadcasted_iota(jnp.int32, (L, L), 0)
    s_idx = lax.broadcasted_iota(jnp.int32, (L, L), 1)
    causal = s_idx <= t_idx
    heads_out = [[None] * n_sub for _ in range(A_HEADS)]
    units = [(sub, h) for sub in range(n_sub) for h in range(A_HEADS)]
    stage = []

    for sub, h in units:
        r0, r1 = sub * L, (sub + 1) * L
        if h == 0:
            gcol = gc_ref[sub] + bc_ref[...]
            gcol = GATE_SOFTCAP * jnp.tanh(gcol / GATE_SOFTCAP)
            grow = gr_ref[sub] + br_ref[...]
            grow = GATE_SOFTCAP * jnp.tanh(grow / GATE_SOFTCAP)
            fcol_all = _log_sigmoid(gcol)
            frow_all = _log_sigmoid(grow)
        q = q_ref[r0:r1, h * A_DK:(h + 1) * A_DK]
        k = k_ref[r0:r1, h * A_DK:(h + 1) * A_DK] * (A_DK ** -0.5)
        v = v_ref[r0:r1, h * A_DV:(h + 1) * A_DV]
        i_col = gcol[:, h:h + 1]
        f_col = fcol_all[:, A_HEADS + h:A_HEADS + h + 1]
        i_row = grow[h:h + 1, :]
        f_row = frow_all[A_HEADS + h:A_HEADS + h + 1, :]
        c_prev = c_ref[sub, h]
        n_prev = n_ref[sub, h:h + 1, :]
        m_prev = m_ref[sub, h:h + 1, 0:1]

        b_col = jnp.sum(jnp.where(causal, f_row, 0.0), axis=1, keepdims=True)
        b_row = jnp.sum(jnp.where(t_idx <= s_idx, f_col, 0.0), axis=0, keepdims=True)

        log_intra = jnp.where(causal, b_col - b_row + i_row, NEG_INF)
        log_inter = b_col + m_prev
        m_t = jnp.maximum(log_inter, jnp.max(log_intra, axis=1, keepdims=True))
        w_intra = jnp.exp(log_intra - m_t)
        w_inter = jnp.exp(log_inter - m_t)

        b_end = b_col[L - 1:L, :]
        log_w_row = b_end - b_row + i_row
        log_w_col = b_end - b_col + i_col
        m_new = jnp.maximum(b_end + m_prev, jnp.max(log_w_row, axis=1, keepdims=True))
        decay = jnp.exp(b_end + m_prev - m_new)
        kw = k * jnp.exp(log_w_col - m_new)

        qb = _mx(q)
        vb = _mx(v)
        stage.append(dict(
            s=_dot_nt(qb, _mx(k)) * w_intra,
            qc=_dot(qb, c_prev.astype(qb.dtype)),
            kv=_dot_tn(_mx(kw), vb),
            qn=jnp.sum(q * n_prev, axis=1, keepdims=True),
            kw_sum=jnp.sum(kw, axis=0, keepdims=True),
            vb=vb, w_inter=w_inter, m_t=m_t, m_new=m_new, decay=decay, c_prev=c_prev, n_prev=n_prev))

    for st in stage:
        st["sv"] = _dot(_mx(st["s"]), st["vb"])

    for (sub, h), st in zip(units, stage):
        r0, r1 = sub * L, (sub + 1) * L
        num = st["w_inter"] * st["qc"] + st["sv"]
        den = st["w_inter"] * st["qn"] + jnp.sum(st["s"], axis=1, keepdims=True)
        hh = num / jnp.maximum(jnp.abs(den), jnp.exp(-st["m_t"]))
        c_ref[sub, h] = st["decay"] * st["c_prev"] + st["kv"]
        n_ref[sub, h:h + 1, :] = st["decay"] * st["n_prev"] + st["kw_sum"]
        m_ref[sub, h:h + 1, :] = jnp.broadcast_to(st["m_new"], (1, LANES))

        hn = hh * lax.rsqrt(jnp.mean(hh * hh, axis=-1, keepdims=True) + EPS)
        hn = hn * gh_ref[:, h * A_DV:(h + 1) * A_DV]
        heads_out[h][sub] = hn * jax.nn.sigmoid(o_ref[r0:r1, h * A_DV:(h + 1) * A_DV])

    for h in range(A_HEADS):
        rows = heads_out[h][0] if n_sub == 1 else jnp.concatenate(heads_out[h], axis=0)
        h_ref[:, h * A_DV:(h + 1) * A_DV] = rows.astype(h_ref.dtype)


def _mlstm(z, zg, b_gate, g_h, *, row0, n_seq, seq_len, chunk, n_sub, t_total, state=None, h_prev=None):
    L = chunk
    nc = seq_len // L
    assert n_sub == 1 or nc == 1
    hd = A_HEADS * A_DK
    hv = A_HEADS * A_DV
    R = n_sub * L
    rb0 = row0 // R
    rows = n_seq * seq_len
    gates = zg[row0:row0 + rows, :2 * A_HEADS].reshape(n_seq * nc, L, 2 * A_HEADS)
    gates_t = jnp.swapaxes(gates, 1, 2)
    has_state = state is not None

    def rmap(col):
        return lambda s, c: (rb0 + s * nc + c, col)

    in_specs = [pl.BlockSpec((R, hd), rmap(0)),
                pl.BlockSpec((R, hd), rmap(1)),
                pl.BlockSpec((R, hv), rmap(hd * 2 // hv)),
                pl.BlockSpec((R, hv), rmap(hd * 2 // hv + 1)),
                pl.BlockSpec((n_sub, L, 2 * A_HEADS), lambda s, c: (s * nc + c, 0, 0)),
                pl.BlockSpec((n_sub, 2 * A_HEADS, L), lambda s, c: (s * nc + c, 0, 0)),
                pl.BlockSpec((1, 2 * A_HEADS), lambda s, c: (0, 0)),
                pl.BlockSpec((2 * A_HEADS, 1), lambda s, c: (0, 0)),
                pl.BlockSpec((1, hv), lambda s, c: (0, 0))]
    args = [z, z, z, z, gates, gates_t, b_gate.reshape(1, -1), b_gate.reshape(-1, 1), g_h.reshape(1, hv)]
    state_specs = [pl.BlockSpec((n_sub, A_HEADS, A_DK, A_DV), lambda s, c: (s, 0, 0, 0)),
                   pl.BlockSpec((n_sub, A_HEADS, A_DK), lambda s, c: (s, 0, 0)),
                   pl.BlockSpec((n_sub, A_HEADS, LANES), lambda s, c: (s, 0, 0))]
    aliases = {}
    if has_state:
        c0, n0, m0 = state
        in_specs += state_specs
        args += [c0, n0, jnp.broadcast_to(m0[..., None], m0.shape + (LANES,))]
        in_specs.append(pl.BlockSpec(memory_space=pl.ANY))
        args.append(h_prev)
        aliases = {len(args) - 1: 0}
    out_shape = (jax.ShapeDtypeStruct((t_total, hv), BF16),
                 jax.ShapeDtypeStruct((n_seq, A_HEADS, A_DK, A_DV), F32),
                 jax.ShapeDtypeStruct((n_seq, A_HEADS, A_DK), F32),
                 jax.ShapeDtypeStruct((n_seq, A_HEADS, LANES), F32))
    h, c, n, m = pl.pallas_call(
        functools.partial(_mlstm_kernel, chunk=L, n_sub=n_sub, has_state=has_state),
        grid=(n_seq // n_sub, nc),
        in_specs=in_specs,
        out_specs=(pl.BlockSpec((R, hv), rmap(0)),) + tuple(state_specs),
        out_shape=out_shape,
        input_output_aliases=aliases,
        compiler_params=_params("parallel", "arbitrary"),
        name="mlstm_state" if has_state else "mlstm",
    )(*args)
    return h, c, n, m[..., 0]


def _alibi_slope(h):
    return 2.0 ** (-8.0 * (h + 1.0) / B_HEADS)


def _sink_softmax(logits, sink):
    m = jnp.maximum(jnp.max(logits, axis=-1, keepdims=True), sink)
    p = jnp.exp(logits - m)
    return p / (jnp.sum(p, axis=-1, keepdims=True) + jnp.exp(sink - m))


def _swa_prompt_kernel(sink_ref, q_ref, kp_ref, ko_ref, vp_ref, vo_ref, gq_ref, gk_ref,
                       o_ref, kn_ref, *, nq):
    W = WINDOW
    t_idx = lax.broadcasted_iota(jnp.int32, (W, 2 * W), 0)
    s_idx = lax.broadcasted_iota(jnp.int32, (W, 2 * W), 1)
    dist = W + t_idx - s_idx
    in_band = (dist >= 0) & (dist <= W)
    in_band_first = in_band & ((s_idx >= W) | (pl.program_id(1) > 0))
    dist_f = dist.astype(F32)
    gq = gq_ref[...]
    gk = gk_ref[...]
    units = [(j, kh) for j in range(nq) for kh in range(B_KV_HEADS)]
    qs, ks, vs, kns = [], [], [], []
    for kh in range(B_KV_HEADS):
        lo, hi = kh * B_HD, (kh + 1) * B_HD
        k_all = jnp.concatenate([kp_ref[:, lo:hi], ko_ref[:, lo:hi]], axis=0)
        v_all = jnp.concatenate([vp_ref[:, lo:hi], vo_ref[:, lo:hi]], axis=0).astype(BF16)
        kn = _rms(k_all, gk)
        kns.append(kn[nq * W:])
        kn = kn.astype(BF16)
        ks.append(kn)
        vs.append(v_all)
    for j, kh in units:
        heads = range(kh * B_GROUP, (kh + 1) * B_GROUP)
        qs.append(jnp.concatenate(
            [_rms(q_ref[j * W:(j + 1) * W, h * B_HD:(h + 1) * B_HD], gq) for h in heads],
            axis=0).astype(BF16))
    scores = [_dot_nt(q, ks[kh][j * W:(j + 2) * W]) for q, (j, kh) in zip(qs, units)]
    probs = []
    for s_all, (j, kh) in zip(scores, units):
        valid = in_band_first if j == 0 else in_band
        pieces = []
        for g in range(B_GROUP):
            h = kh * B_GROUP + g
            s = s_all[g * W:(g + 1) * W] * (B_HD ** -0.5)
            logits = jnp.where(valid, s - _alibi_slope(h) * dist_f, NEG_INF)
            pieces.append(_sink_softmax(logits, sink_ref[h]).astype(BF16))
        probs.append(jnp.concatenate(pieces, axis=0))
    outs = [_dot(p, vs[kh][j * W:(j + 2) * W]) for p, (j, kh) in zip(probs, units)]
    for j in range(nq):
        o_ref[j * W:(j + 1) * W, :] = jnp.concatenate(
            [outs[j * B_KV_HEADS + kh][g * W:(g + 1) * W]
             for kh in range(B_KV_HEADS) for g in range(B_GROUP)], axis=-1).astype(o_ref.dtype)
    kn_ref[0] = jnp.concatenate(kns, axis=-1)


def _swa_prompt(z, g_q, g_k, sinks, *, n_seq, seq_len, t_total, nq):
    W = WINDOW
    nb = seq_len // (nq * W)
    bq = B_HEADS * B_HD
    bkv = B_KV_HEADS * B_HD
    kcol = bq // bkv

    def own(col):
        return lambda b, i: (b * nb + i, col)

    def prev(col):
        return lambda b, i: (jnp.maximum((b * nb + i) * nq - 1, 0), col)

    return pl.pallas_call(
        functools.partial(_swa_prompt_kernel, nq=nq),
        grid=(n_seq, nb),
        in_specs=[pl.BlockSpec(memory_space=pltpu.SMEM),
                  pl.BlockSpec((nq * W, bq), own(0)),
                  pl.BlockSpec((W, bkv), prev(kcol)),
                  pl.BlockSpec((nq * W, bkv), own(kcol)),
                  pl.BlockSpec((W, bkv), prev(kcol + 1)),
                  pl.BlockSpec((nq * W, bkv), own(kcol + 1)),
                  pl.BlockSpec((1, B_HD), lambda b, i: (0, 0)),
                  pl.BlockSpec((1, B_HD), lambda b, i: (0, 0))],
        out_specs=(pl.BlockSpec((nq * W, bq), own(0)),
                   pl.BlockSpec((1, W, bkv), lambda b, i: (b, 0, 0))),
        out_shape=(jax.ShapeDtypeStruct((t_total, bq), BF16),
                   jax.ShapeDtypeStruct((n_seq, W, bkv), F32)),
        compiler_params=_params("parallel", "arbitrary"),
        name="swa_prompt",
    )(sinks, z, z, z, z, z, g_q.reshape(1, B_HD), g_k.reshape(1, B_HD))


def _swa_sample_kernel(sink_ref, q_ref, k_ref, v_ref, ck_ref, cv_ref, gq_ref, gk_ref, _,
                       o_ref, cko_ref, cvo_ref, *, n_new, n_batch):
    W = WINDOW
    T = n_new
    R = B_HEADS * T
    GT = B_GROUP * T
    r_col = lax.broadcasted_iota(jnp.int32, (R, 1), 0)
    s_idx = lax.broadcasted_iota(jnp.int32, (R, W + T), 1)
    t_col = r_col
    slope_col = jnp.full((R, 1), _alibi_slope(0), F32)
    sink_col = jnp.full((R, 1), sink_ref[0], F32)
    for h in range(1, B_HEADS):
        in_later_head = r_col >= h * T
        t_col = jnp.where(in_later_head, r_col - h * T, t_col)
        slope_col = jnp.where(in_later_head, _alibi_slope(h), slope_col)
        sink_col = jnp.where(in_later_head, sink_ref[h], sink_col)
    dist = W + t_col - s_idx
    valid = (dist >= 0) & (dist <= W)
    bias = slope_col * dist.astype(F32)
    gq = gq_ref[...]
    gk = gk_ref[...]
    qs, ks, vs = [], [], []
    for bi in range(n_batch):
        r0, r1 = bi * T, (bi + 1) * T
        for kh in range(B_KV_HEADS):
            lo, hi = kh * B_HD, (kh + 1) * B_HD
            kn_new = _rms(k_ref[r0:r1, lo:hi], gk)
            k_all = jnp.concatenate([ck_ref[bi, :, lo:hi], kn_new], axis=0)
            v_all = jnp.concatenate([cv_ref[bi, :, lo:hi], v_ref[r0:r1, lo:hi]], axis=0)
            cko_ref[bi, :, lo:hi] = k_all[T:]
            cvo_ref[bi, :, lo:hi] = v_all[T:]
            ks.append(k_all)
            vs.append(v_all)
            heads = range(kh * B_GROUP, (kh + 1) * B_GROUP)
            qs.append(jnp.concatenate(
                [_rms(q_ref[r0:r1, h * B_HD:(h + 1) * B_HD], gq) for h in heads], axis=0))
    scores = [_dot_nt(q, k) for q, k in zip(qs, ks)]
    probs = []
    for bi in range(n_batch):
        s = jnp.concatenate(scores[bi * B_KV_HEADS:(bi + 1) * B_KV_HEADS], axis=0)
        logits = jnp.where(valid, s * (B_HD ** -0.5) - bias, NEG_INF)
        probs.append(_sink_softmax(logits, sink_col))
    outs = [_dot(probs[i // B_KV_HEADS][(i % B_KV_HEADS) * GT:(i % B_KV_HEADS + 1) * GT], v)
            for i, v in enumerate(vs)]
    rows_out = []
    for bi in range(n_batch):
        rows_out.append(jnp.concatenate(
            [outs[bi * B_KV_HEADS + kh][g * T:(g + 1) * T]
             for kh in range(B_KV_HEADS) for g in range(B_GROUP)], axis=-1))
    o_ref[...] = jnp.concatenate(rows_out, axis=0).astype(o_ref.dtype)


def _swa_sample(z, cache_k, cache_v, g_q, g_k, sinks, o_prev, *, row0, n_new, n_batch):
    W = WINDOW
    n_seq = cache_k.shape[0]
    bq = B_HEADS * B_HD
    bkv = B_KV_HEADS * B_HD
    kcol = bq // bkv
    rows = n_new * n_batch
    rb0 = row0 // rows
    ck = cache_k.reshape(n_seq, W, bkv)
    cv = cache_v.reshape(n_seq, W, bkv)
    cache_spec = pl.BlockSpec((n_batch, W, bkv), lambda i: (i, 0, 0))
    o, cko, cvo = pl.pallas_call(
        functools.partial(_swa_sample_kernel, n_new=n_new, n_batch=n_batch),
        grid=(n_seq // n_batch,),
        in_specs=[pl.BlockSpec(memory_space=pltpu.SMEM),
                  pl.BlockSpec((rows, bq), lambda i: (rb0 + i, 0)),
                  pl.BlockSpec((rows, bkv), lambda i: (rb0 + i, kcol)),
                  pl.BlockSpec((rows, bkv), lambda i: (rb0 + i, kcol + 1)),
                  cache_spec, cache_spec,
                  pl.BlockSpec((1, B_HD), lambda i: (0, 0)),
                  pl.BlockSpec((1, B_HD), lambda i: (0, 0)),
                  pl.BlockSpec(memory_space=pl.ANY)],
        out_specs=(pl.BlockSpec((rows, bq), lambda i: (rb0 + i, 0)), cache_spec, cache_spec),
        out_shape=(jax.ShapeDtypeStruct(o_prev.shape, o_prev.dtype),
                   jax.ShapeDtypeStruct(ck.shape, cache_k.dtype),
                   jax.ShapeDtypeStruct(cv.shape, cache_v.dtype)),
        input_output_aliases={8: 0},
        compiler_params=_params("parallel"),
        name="swa_sample",
    )(sinks, z, z, z, ck, cv, g_q.reshape(1, B_HD), g_k.reshape(1, B_HD), o_prev)
    return o, cko.reshape(cache_k.shape), cvo.reshape(cache_v.shape)


def _top_ranked(s, k):
    rows = s.shape[0]
    ridx = lax.broadcasted_iota(jnp.int32, s.shape, 0).astype(F32)
    rank = jnp.full(s.shape, float(k), F32)
    vals = []
    for r in range(k):
        m = jnp.max(s, axis=0, keepdims=True)
        vals.append(m)
        first = ridx == jnp.min(jnp.where(s == m, ridx, float(rows)), axis=0, keepdims=True)
        rank = jnp.where(first, float(r), rank)
        s = jnp.where(first, NEG_INF, s)
    return vals, rank


def _top_ranked_distinct(s, k):
    rank = jnp.full(s.shape, float(k), F32)
    vals = []
    for r in range(k):
        m = jnp.max(s, axis=0, keepdims=True)
        vals.append(m)
        hit = s == m
        rank = jnp.where(hit, float(r), rank)
        s = jnp.where(hit, NEG_INF, s)
    return vals, rank


def _ranked_ok(rank, k):
    n_ranked = jnp.sum(jnp.where(rank < k, 1.0, 0.0), axis=0, keepdims=True)
    return jnp.where(n_ranked == k, 1.0, 0.0)


_CAND_PAIRS = [(a, b) for a in range(P_TOPK) for b in range(P_TOPK) if (a + 1) * (b + 1) <= P_TOPK]
_CAND_ROWS = -(-len(_CAND_PAIRS) // 8) * 8


TAB_COUNT0, TAB_GATE0 = 0, 1
TAB_RANK1, TAB_GATE1 = 0, 1


def _peer_route_kernel(x_ref, g_ref, wqt_ref, keys_ref, xnt_ref, tab0_ref, tab1_ref,
                       qt_ref, cand_ref, tied_ref):
    xn = _rms(x_ref[...], g_ref[...])
    xnt = xn.T.astype(BF16)
    xnt_ref[...] = xnt
    qt_ref[...] = _dot(wqt_ref[...], xnt)
    half = keys_ref.shape[2]
    cand_ref[...] = jnp.full(cand_ref.shape, NEG_INF, F32)

    def route_head(h, top_fn):
        row0 = 2 * h * half if isinstance(h, int) else pl.multiple_of(2 * h * half, half)
        q0 = qt_ref[pl.ds(row0, half), :]
        q1 = qt_ref[pl.ds(row0 + half, half), :]
        s0 = _dot(keys_ref[2 * h], q0)
        s1 = _dot(keys_ref[2 * h + 1], q1)
        top0, rank0 = top_fn(s0, P_TOPK)
        top1, rank1 = top_fn(s1, P_TOPK)
        for r, (a, b) in enumerate(_CAND_PAIRS):
            cand_ref[r:r + 1, :] = top0[a] + top1[b]
        best, cand_rank = top_fn(cand_ref[...], P_TOPK)
        z = jnp.zeros_like(best[0])
        for c in best:
            z = z + jnp.exp(c - best[0])
        taken = jnp.where(cand_rank < P_TOPK, 1.0, 0.0)
        count0 = jnp.zeros_like(s0)
        for a in range(P_TOPK):
            rows = [r for r, (ca, _) in enumerate(_CAND_PAIRS) if ca == a]
            n_sel = jnp.sum(taken[rows[0]:rows[-1] + 1], axis=0, keepdims=True)
            count0 = jnp.where(rank0 == a, n_sel, count0)
        tab0_ref[2 * h + TAB_COUNT0] = 2.0 * count0 - 1.0
        tab0_ref[2 * h + TAB_GATE0] = jnp.exp(s0 - top0[0]) / z
        tab1_ref[2 * h + TAB_RANK1] = (2.0 * rank1).astype(BF16)
        tab1_ref[2 * h + TAB_GATE1] = jnp.exp(s1 - top1[0]).astype(BF16)
        return _ranked_ok(rank0, P_TOPK) * _ranked_ok(rank1, P_TOPK) * _ranked_ok(cand_rank, P_TOPK)

    for h in range(P_HEADS):
        ok = route_head(h, _top_ranked_distinct)
        tied_ref[h] = (jnp.min(ok) < 1.0).astype(jnp.int32)

    def redo(h, carry):
        @pl.when(tied_ref[h] != 0)
        def _():
            route_head(h, _top_ranked)
        return carry

    lax.fori_loop(0, P_HEADS, redo, 0)


def _peer_route(x, g, wqt, keys, *, tt):
    t, d = x.shape
    nq = wqt.shape[0]
    hp, nk, half = keys.shape
    tab_spec = pl.BlockSpec((2 * P_HEADS, nk, tt), lambda i: (0, 0, i))
    return pl.pallas_call(
        _peer_route_kernel,
        grid=(t // tt,),
        in_specs=[pl.BlockSpec((tt, d), lambda i: (i, 0)),
                  pl.BlockSpec((1, d), lambda i: (0, 0)),
                  pl.BlockSpec((nq, d), lambda i: (0, 0)),
                  pl.BlockSpec((hp, nk, half), lambda i: (0, 0, 0))],
        out_specs=(pl.BlockSpec((d, tt), lambda i: (0, i)), tab_spec, tab_spec),
        out_shape=(jax.ShapeDtypeStruct((d, t), BF16),
                   jax.ShapeDtypeStruct((2 * P_HEADS, nk, t), F32),
                   jax.ShapeDtypeStruct((2 * P_HEADS, nk, t), BF16)),
        scratch_shapes=[pltpu.VMEM((nq, tt), F32), pltpu.VMEM((_CAND_ROWS, tt), F32),
                        pltpu.SMEM((P_HEADS,), jnp.int32)],
        compiler_params=_params("parallel"),
        name="peer_route",
    )(x, g.reshape(1, d), wqt, keys)


def _gelu(x):
    return 0.5 * x * (1.0 + lax.erf(x * math.sqrt(0.5)))


def _rows_bf16(row, n):
    packed = jnp.broadcast_to(row, (BF16_SUBLANES, LANES)).astype(BF16)
    return jnp.concatenate([packed] * (n // BF16_SUBLANES), axis=0)


def _peer_expert_kernel(xnt_ref, tab0_ref, tab1_ref, u_ref, vt_ref, yt_ref, w0_ref, *, ne, nc):
    et = pl.program_id(1)
    n_et = pl.num_programs(1)
    tt = xnt_ref.shape[1]
    n_i = ne // N_KEYS
    assert n_i == SUBLANES and nc % LANES == 0
    n_chunk = tt // nc
    row_group = 4
    blocks_rc = [(r, c) for r in range(n_i) for c in range(nc // LANES)]

    def hidden(k):
        return _dot(u_ref[...], xnt_ref[:, k * nc:(k + 1) * nc])

    def gates(k, tile):
        rows8 = pl.ds(pl.multiple_of(tile * n_i, SUBLANES), SUBLANES)
        blocks = {}
        for c in range(nc // LANES):
            cs = slice(k * nc + c * LANES, k * nc + (c + 1) * LANES)
            for r0 in range(0, n_i, row_group):
                rows = range(r0, r0 + row_group)
                w = {r: jnp.zeros((N_KEYS, LANES), BF16) for r in rows}
                for h in range(P_HEADS):
                    rank1 = tab1_ref[2 * h + TAB_RANK1, :, cs]
                    gate1 = tab1_ref[2 * h + TAB_GATE1, :, cs]
                    count0 = tab0_ref[2 * h + TAB_COUNT0, rows8, cs]
                    gate0 = tab0_ref[2 * h + TAB_GATE0, rows8, cs]
                    for r in rows:
                        sel_gate1 = jnp.minimum(
                            jnp.maximum(_rows_bf16(count0[r:r + 1], N_KEYS) - rank1, 0.0), gate1)
                        w[r] = w[r] + sel_gate1 * _rows_bf16(gate0[r:r + 1], N_KEYS)
                for r in rows:
                    blocks[r, c] = w[r]
        return blocks

    def activate(hid, w):
        return jnp.concatenate(
            [jnp.concatenate(
                [_gelu(hid[r * N_KEYS:(r + 1) * N_KEYS, c * LANES:(c + 1) * LANES]).astype(BF16) * w[r, c]
                 for c in range(nc // LANES)], axis=1)
             for r in range(n_i)], axis=0)

    def accumulate(k, a):
        yt_ref[:, k * nc:(k + 1) * nc] += _dot(vt_ref[...], a)

    def store_gates(w):
        for r, c in blocks_rc:
            w0_ref[r * N_KEYS:(r + 1) * N_KEYS, c * LANES:(c + 1) * LANES] = w[r, c]

    @pl.when(et == 0)
    def _():
        yt_ref[...] = jnp.zeros_like(yt_ref)
        store_gates(gates(0, et))

    w = {(r, c): w0_ref[r * N_KEYS:(r + 1) * N_KEYS, c * LANES:(c + 1) * LANES] for r, c in blocks_rc}
    hid = hidden(0)
    a_prev = None
    for k in range(n_chunk):
        last = k + 1 == n_chunk
        hid_next = None if last else hidden(k + 1)
        if a_prev is not None:
            accumulate(k - 1, a_prev)
        w_next = None if last else gates(k + 1, et)
        a_prev = activate(hid, w)
        hid, w = hid_next, w_next
    w_ahead = gates(0, jnp.minimum(et + 1, n_et - 1))
    accumulate(n_chunk - 1, a_prev)
    store_gates(w_ahead)


def _peer_experts(xnt, tab0, tab1, u, vt, *, tt, ne, nc):
    d, t = xnt.shape
    n_tab, nk, _ = tab0.shape
    n_exp = u.shape[0]
    once = pl.Buffered(1)
    tab_spec = pl.BlockSpec((n_tab, nk, tt), lambda i, e: (0, 0, i), pipeline_mode=once)
    return pl.pallas_call(
        functools.partial(_peer_expert_kernel, ne=ne, nc=nc),
        grid=(t // tt, n_exp // ne),
        in_specs=[pl.BlockSpec((d, tt), lambda i, e: (0, i), pipeline_mode=once),
                  tab_spec, tab_spec,
                  pl.BlockSpec((ne, d), lambda i, e: (e, 0)),
                  pl.BlockSpec((d, ne), lambda i, e: (0, e))],
        out_specs=pl.BlockSpec((d, tt), lambda i, e: (0, i)),
        out_shape=jax.ShapeDtypeStruct((d, t), F32),
        scratch_shapes=[pltpu.VMEM((ne, nc), BF16)],
        compiler_params=_params("parallel", "arbitrary"),
        name="peer_experts",
    )(xnt, tab0, tab1, u, vt)


def _add_transposed_kernel(x_ref, yt_ref, o_ref):
    o_ref[...] = x_ref[...] + yt_ref[...].T


def _add_transposed(x, yt, *, tt, row0=0, rows=None):
    t, d = x.shape
    rows = t if rows is None else rows
    b0 = row0 // tt
    return pl.pallas_call(
        _add_transposed_kernel,
        grid=(rows // tt,),
        in_specs=[pl.BlockSpec((tt, d), lambda i: (b0 + i, 0)),
                  pl.BlockSpec((d, tt), lambda i: (0, b0 + i))],
        out_specs=pl.BlockSpec((tt, d), lambda i: (i, 0)),
        out_shape=jax.ShapeDtypeStruct((rows, d), F32),
        compiler_params=_params("parallel"),
        name="add_transposed",
    )(x, yt)


def _cast_kernel(x_ref, o_ref, *, transpose):
    x = x_ref[...]
    o_ref[...] = (x.T if transpose else x).astype(o_ref.dtype)


def _layer_cast(w, layer, dtype, *, rows, transpose):
    _, n, d = w.shape
    return pl.pallas_call(
        functools.partial(_cast_kernel, transpose=transpose),
        grid=(n // rows,),
        in_specs=[pl.BlockSpec((None, rows, d), lambda i: (layer, i, 0))],
        out_specs=(pl.BlockSpec((d, rows), lambda i: (0, i)) if transpose
                   else pl.BlockSpec((rows, d), lambda i: (i, 0))),
        out_shape=jax.ShapeDtypeStruct((d, n) if transpose else (n, d), dtype),
        compiler_params=_params("parallel"),
        name="layer_cast_t" if transpose else "layer_cast",
    )(w)


def _peer(x, g, w_q, keys, peer_u, peer_v, layer, *, tt_route, tt, ne):
    hp = keys.shape[0] * keys.shape[1]
    xnt, tab0, tab1 = _peer_route(x, g, w_q.T.astype(BF16),
                                  keys.reshape(hp, keys.shape[2], keys.shape[3]), tt=tt_route)
    u = _layer_cast(peer_u, layer, BF16, rows=512, transpose=False)
    vt = _layer_cast(peer_v, layer, BF16, rows=512, transpose=True)
    return _peer_experts(xnt, tab0, tab1, u, vt, tt=tt, ne=ne, nc=min(tt, 2 * LANES))


def _pick_tile(n, pref):
    t = pref
    while n % t:
        t //= 2
    return t


def kernel(x_prompt, x_sample, state_c, state_n, state_m, cache_k, cache_v, norm_mix, w_in_a, b_gate_a,
           norm_h_a, w_out_a, w_in_b, g_q_b, g_k_b, sink_b, w_out_b, norm_ffn, peer_wq, peer_keys,
           peer_u, peer_v):
    bp, sp, d = x_prompt.shape
    bs, ss, _ = x_sample.shape
    tp, ts = bp * sp, bs * ss
    t = tp + ts
    x = jnp.concatenate([x_prompt.reshape(tp, d), x_sample.reshape(ts, d)], axis=0)
    tm = _pick_tile(math.gcd(tp, ts), 512)
    tr = min(tm, 256)
    te = _pick_tile(t, 1024)
    n_qkvo = 2 * A_HEADS * (A_DK + A_DV)
    chunk_p = _pick_tile(sp, 256)

    w_in = w_in_a[0]
    z = _norm_matmul(x, norm_mix[0], w_in[:, :n_qkvo].astype(BF16), tm=tm, tn=1024)
    w_gate = jnp.pad(w_in[:, n_qkvo:], ((0, 0), (0, LANES - 2 * A_HEADS)))
    zg = _norm_matmul(x, norm_mix[0], w_gate, tm=tm, tn=LANES, precise=True)
    h, c_p, n_p, m_p = _mlstm(z, zg, b_gate_a[0], norm_h_a[0], row0=0, n_seq=bp, seq_len=sp,
                              chunk=chunk_p, n_sub=1, t_total=t)
    h, c_s, n_s, m_s = _mlstm(z, zg, b_gate_a[0], norm_h_a[0], row0=tp, n_seq=bs, seq_len=ss,
                              chunk=ss, n_sub=4, t_total=t,
                              state=(state_c[0], state_n[0], state_m[0]), h_prev=h)
    x = _matmul_res(h, w_out_a[0].astype(BF16), x, tm=tm, tn=1024)
    yt = _peer(x, norm_ffn[0], peer_wq[0], peer_keys[0], peer_u, peer_v, 0, tt_route=tr, tt=te, ne=1024)
    x = _add_transposed(x, yt, tt=tr)

    z = _norm_matmul(x, norm_mix[1], w_in_b[0].astype(BF16), tm=tm, tn=1024)
    o, kn_p = _swa_prompt(z, g_q_b[0], g_k_b[0], sink_b[0], n_seq=bp, seq_len=sp, t_total=t,
                          nq=2 if sp % (2 * WINDOW) == 0 else 1)
    o, ck_s, cv_s = _swa_sample(z, cache_k[0], cache_v[0], g_q_b[0], g_k_b[0], sink_b[0], o,
                                row0=tp, n_new=ss, n_batch=8)
    x = _matmul_res(o, w_out_b[0].astype(BF16), x, tm=tm, tn=1024)
    yt = _peer(x, norm_ffn[1], peer_wq[1], peer_keys[1], peer_u, peer_v, 1, tt_route=tr, tt=te, ne=1024)
    y_p = _add_transposed(x, yt, tt=tr, row0=0, rows=tp)
    y_s = _add_transposed(x, yt, tt=tr, row0=tp, rows=ts)

    bq = B_HEADS * B_HD
    bkv = B_KV_HEADS * B_HD
    v_p = z[:tp, bq + bkv:].reshape(bp, sp, bkv)[:, sp - WINDOW:]
    cache_shape = (1, bp, WINDOW, B_KV_HEADS, B_HD)
    return (y_p.reshape(bp, sp, d), y_s.reshape(bs, ss, d),
            c_p[None], n_p[None], m_p[None],
            kn_p.reshape(cache_shape), v_p.reshape(cache_shape),
            c_s[None], n_s[None], m_s[None], ck_s[None], cv_s[None])
```

```python
import functools
import math

import jax
import jax.numpy as jnp
from jax import lax
from jax.experimental import pallas as pl
from jax.experimental.pallas import tpu as pltpu

F32 = jnp.float32
BF16 = jnp.bfloat16
EPS = 1e-6
NEG_INF = float("-inf")

A_HEADS = 8
A_DK = 128
A_DV = 256
GATE_SOFTCAP = 15.0
B_HEADS = 32
B_KV_HEADS = 8
B_HD = 64
B_GROUP = B_HEADS // B_KV_HEADS
WINDOW = 128
P_HEADS = 8
N_KEYS = 128
P_TOPK = 16

LANES = 128
SUBLANES = 8
BF16_SUBLANES = 16
VMEM_LIMIT = 58 * 1024 * 1024


def _params(*sem):
    return pltpu.CompilerParams(dimension_semantics=sem, vmem_limit_bytes=VMEM_LIMIT)


def _dot(a, b):
    return jnp.dot(a, b, preferred_element_type=F32)


def _dot_nt(a, b):
    return lax.dot_general(a, b, (((1,), (1,)), ((), ())), preferred_element_type=F32)


def _dot_tn(a, b):
    return lax.dot_general(a, b, (((0,), (0,)), ((), ())), preferred_element_type=F32)


def _rms(x, g):
    return x * lax.rsqrt(jnp.mean(x * x, axis=-1, keepdims=True) + EPS) * g


def _rms_all(xs, g):
    ms = [jnp.mean(x * x, axis=-1, keepdims=True) for x in xs]
    return [x * lax.rsqrt(m + EPS) * g for x, m in zip(xs, ms)]


def _norm_matmul_kernel(x_ref, g_ref, w_ref, o_ref, xn_ref, *, precise):
    @pl.when(pl.program_id(1) == 0)
    def _():
        xn_ref[...] = _rms(x_ref[...], g_ref[...]).astype(xn_ref.dtype)

    if precise:
        o_ref[...] = jnp.dot(xn_ref[...], w_ref[...], preferred_element_type=F32,
                             precision=lax.Precision.HIGHEST)
    else:
        o_ref[...] = _dot(xn_ref[...], w_ref[...])


def _norm_matmul(x, g, w, *, tm, tn, precise=False):
    t, d = x.shape
    n = w.shape[1]
    return pl.pallas_call(
        functools.partial(_norm_matmul_kernel, precise=precise),
        grid=(t // tm, n // tn),
        in_specs=[pl.BlockSpec((tm, d), lambda i, j: (i, 0)),
                  pl.BlockSpec((1, d), lambda i, j: (0, 0)),
                  pl.BlockSpec((d, tn), lambda i, j: (0, j))],
        out_specs=pl.BlockSpec((tm, tn), lambda i, j: (i, j)),
        out_shape=jax.ShapeDtypeStruct((t, n), F32),
        scratch_shapes=[pltpu.VMEM((tm, d), F32 if precise else BF16)],
        compiler_params=_params("parallel", "arbitrary"),
        name="norm_matmul_f32" if precise else "norm_matmul",
    )(x, g.reshape(1, d), w)


def _matmul_res_kernel(a_ref, w_ref, r_ref, o_ref):
    o_ref[...] = r_ref[...] + _dot(a_ref[...], w_ref[...])


def _matmul_res(a, w, r, *, tm, tn):
    t, k = a.shape
    n = w.shape[1]
    return pl.pallas_call(
        _matmul_res_kernel,
        grid=(t // tm, n // tn),
        in_specs=[pl.BlockSpec((tm, k), lambda i, j: (i, 0)),
                  pl.BlockSpec((k, tn), lambda i, j: (0, j)),
                  pl.BlockSpec((tm, tn), lambda i, j: (i, j))],
        out_specs=pl.BlockSpec((tm, tn), lambda i, j: (i, j)),
        out_shape=jax.ShapeDtypeStruct((t, n), F32),
        compiler_params=_params("parallel", "arbitrary"),
        name="matmul_res",
    )(a, w, r)


def _log_sigmoid(x):
    return jnp.minimum(x, 0.0) - jnp.log(1.0 + jnp.exp(-jnp.abs(x)))


def _mx(x):
    return x.astype(BF16) if x.shape[0] % 16 == 0 else x


def _mlstm_kernel(*refs, chunk, n_sub, has_state):
    if has_state:
        (q_ref, k_ref, v_ref, o_ref, gc_ref, gr_ref, bc_ref, br_ref, gh_ref,
         c0_ref, n0_ref, m0_ref, _, h_ref, c_ref, n_ref, m_ref) = refs
    else:
        (q_ref, k_ref, v_ref, o_ref, gc_ref, gr_ref, bc_ref, br_ref, gh_ref,
         h_ref, c_ref, n_ref, m_ref) = refs
    L = chunk

    @pl.when(pl.program_id(1) == 0)
    def _():
        if has_state:
            c_ref[...] = c0_ref[...]
            n_ref[...] = n0_ref[...]
            m_ref[...] = m0_ref[...]
        else:
            c_ref[...] = jnp.zeros_like(c_ref)
            n_ref[...] = jnp.zeros_like(n_ref)
            m_ref[...] = jnp.zeros_like(m_ref)

    t_idx = lax.broadcasted_iota(jnp.int32, (L, L), 0)
    s_idx = lax.broadcasted_iota(jnp.int32, (L, L), 1)
    causal = s_idx <= t_idx
    heads_out = [[None] * n_sub for _ in range(A_HEADS)]
    units = [(sub, h) for sub in range(n_sub) for h in range(A_HEADS)]
    st = [dict() for _ in units]

    for u, (sub, h) in zip(st, units):
        r0, r1 = sub * L, (sub + 1) * L
        if h == 0:
            gcol = gc_ref[sub] + bc_ref[...]
            gcol = GATE_SOFTCAP * jnp.tanh(gcol / GATE_SOFTCAP)
            grow = gr_ref[sub] + br_ref[...]
            grow = GATE_SOFTCAP * jnp.tanh(grow / GATE_SOFTCAP)
            fcol_all = _log_sigmoid(gcol)
            frow_all = _log_sigmoid(grow)
        u["q"] = q_ref[r0:r1, h * A_DK:(h + 1) * A_DK]
        u["k"] = k_ref[r0:r1, h * A_DK:(h + 1) * A_DK] * (A_DK ** -0.5)
        u["vb"] = _mx(v_ref[r0:r1, h * A_DV:(h + 1) * A_DV])
        u["i_col"] = gcol[:, h:h + 1]
        u["i_row"] = grow[h:h + 1, :]
        f_col = fcol_all[:, A_HEADS + h:A_HEADS + h + 1]
        f_row = frow_all[A_HEADS + h:A_HEADS + h + 1, :]
        u["c_prev"] = c_ref[sub, h]
        u["n_prev"] = n_ref[sub, h:h + 1, :]
        u["m_prev"] = m_ref[sub, h:h + 1, 0:1]
        u["b_col"] = jnp.sum(jnp.where(causal, f_row, 0.0), axis=1, keepdims=True)
        u["b_row"] = jnp.sum(jnp.where(t_idx <= s_idx, f_col, 0.0), axis=0, keepdims=True)
        qb = _mx(u["q"])
        u["qk"] = _dot_nt(qb, _mx(u["k"]))
        u["qc"] = _dot(qb, u["c_prev"].astype(qb.dtype))
        u["qn"] = jnp.sum(u["q"] * u["n_prev"], axis=1, keepdims=True)

    for u in st:
        b_end = u["b_col"][L - 1:L, :]
        u["log_intra"] = jnp.where(causal, u["b_col"] - u["b_row"] + u["i_row"], NEG_INF)
        u["log_inter"] = u["b_col"] + u["m_prev"]
        u["max_intra"] = jnp.max(u["log_intra"], axis=1, keepdims=True)
        u["log_w_col"] = b_end - u["b_col"] + u["i_col"]
        u["m_end"] = b_end + u["m_prev"]
        u["max_w"] = jnp.max(b_end - u["b_row"] + u["i_row"], axis=1, keepdims=True)

    for u in st:
        u["m_t"] = jnp.maximum(u["log_inter"], u["max_intra"])
        u["w_inter"] = jnp.exp(u["log_inter"] - u["m_t"])
        u["s"] = u["qk"] * jnp.exp(u["log_intra"] - u["m_t"])
        u["s_sum"] = jnp.sum(u["s"], axis=1, keepdims=True)
        u["m_new"] = jnp.maximum(u["m_end"], u["max_w"])
        u["decay"] = jnp.exp(u["m_end"] - u["m_new"])
        kw = u["k"] * jnp.exp(u["log_w_col"] - u["m_new"])
        u["kw_sum"] = jnp.sum(kw, axis=0, keepdims=True)
        u["sv"] = _dot(_mx(u["s"]), u["vb"])
        u["kv"] = _dot_tn(_mx(kw), u["vb"])

    for u, (sub, h) in zip(st, units):
        num = u["w_inter"] * u["qc"] + u["sv"]
        den = u["w_inter"] * u["qn"] + u["s_sum"]
        u["hh"] = num / jnp.maximum(jnp.abs(den), jnp.exp(-u["m_t"]))
        u["hh_ms"] = jnp.mean(u["hh"] * u["hh"], axis=-1, keepdims=True)
        c_ref[sub, h] = u["decay"] * u["c_prev"] + u["kv"]
        n_ref[sub, h:h + 1, :] = u["decay"] * u["n_prev"] + u["kw_sum"]
        m_ref[sub, h:h + 1, :] = jnp.broadcast_to(u["m_new"], (1, LANES))

    for u, (sub, h) in zip(st, units):
        r0, r1 = sub * L, (sub + 1) * L
        hn = u["hh"] * lax.rsqrt(u["hh_ms"] + EPS) * gh_ref[:, h * A_DV:(h + 1) * A_DV]
        heads_out[h][sub] = hn * jax.nn.sigmoid(o_ref[r0:r1, h * A_DV:(h + 1) * A_DV])

    for h in range(A_HEADS):
        rows = heads_out[h][0] if n_sub == 1 else jnp.concatenate(heads_out[h], axis=0)
        h_ref[:, h * A_DV:(h + 1) * A_DV] = rows.astype(h_ref.dtype)


def _mlstm(z, zg, b_gate, g_h, *, row0, n_seq, seq_len, chunk, n_sub, t_total, state=None, h_prev=None):
    L = chunk
    nc = seq_len // L
    assert n_sub == 1 or nc == 1
    hd = A_HEADS * A_DK
    hv = A_HEADS * A_DV
    R = n_sub * L
    rb0 = row0 // R
    rows = n_seq * seq_len
    gates = zg[row0:row0 + rows, :2 * A_HEADS].reshape(n_seq * nc, L, 2 * A_HEADS)
    gates_t = jnp.swapaxes(gates, 1, 2)
    has_state = state is not None

    def rmap(col):
        return lambda s, c: (rb0 + s * nc + c, col)

    in_specs = [pl.BlockSpec((R, hd), rmap(0)),
                pl.BlockSpec((R, hd), rmap(1)),
                pl.BlockSpec((R, hv), rmap(hd * 2 // hv)),
                pl.BlockSpec((R, hv), rmap(hd * 2 // hv + 1)),
                pl.BlockSpec((n_sub, L, 2 * A_HEADS), lambda s, c: (s * nc + c, 0, 0)),
                pl.BlockSpec((n_sub, 2 * A_HEADS, L), lambda s, c: (s * nc + c, 0, 0)),
                pl.BlockSpec((1, 2 * A_HEADS), lambda s, c: (0, 0)),
                pl.BlockSpec((2 * A_HEADS, 1), lambda s, c: (0, 0)),
                pl.BlockSpec((1, hv), lambda s, c: (0, 0))]
    args = [z, z, z, z, gates, gates_t, b_gate.reshape(1, -1), b_gate.reshape(-1, 1), g_h.reshape(1, hv)]
    state_specs = [pl.BlockSpec((n_sub, A_HEADS, A_DK, A_DV), lambda s, c: (s, 0, 0, 0)),
                   pl.BlockSpec((n_sub, A_HEADS, A_DK), lambda s, c: (s, 0, 0)),
                   pl.BlockSpec((n_sub, A_HEADS, LANES), lambda s, c: (s, 0, 0))]
    aliases = {}
    if has_state:
        c0, n0, m0 = state
        in_specs += state_specs
        args += [c0, n0, jnp.broadcast_to(m0[..., None], m0.shape + (LANES,))]
        in_specs.append(pl.BlockSpec(memory_space=pl.ANY))
        args.append(h_prev)
        aliases = {len(args) - 1: 0}
    out_shape = (jax.ShapeDtypeStruct((t_total, hv), BF16),
                 jax.ShapeDtypeStruct((n_seq, A_HEADS, A_DK, A_DV), F32),
                 jax.ShapeDtypeStruct((n_seq, A_HEADS, A_DK), F32),
                 jax.ShapeDtypeStruct((n_seq, A_HEADS, LANES), F32))
    h, c, n, m = pl.pallas_call(
        functools.partial(_mlstm_kernel, chunk=L, n_sub=n_sub, has_state=has_state),
        grid=(n_seq // n_sub, nc),
        in_specs=in_specs,
        out_specs=(pl.BlockSpec((R, hv), rmap(0)),) + tuple(state_specs),
        out_shape=out_shape,
        input_output_aliases=aliases,
        compiler_params=_params("parallel", "arbitrary"),
        name="mlstm_state" if has_state else "mlstm",
    )(*args)
    return h, c, n, m[..., 0]


def _alibi_slope(h):
    return 2.0 ** (-8.0 * (h + 1.0) / B_HEADS)


def _sink_softmax_all(logits, sinks):
    ms = [jnp.maximum(jnp.max(l, axis=-1, keepdims=True), s) for l, s in zip(logits, sinks)]
    ps = [jnp.exp(l - m) for l, m in zip(logits, ms)]
    dens = [jnp.sum(p, axis=-1, keepdims=True) + jnp.exp(s - m) for p, s, m in zip(ps, sinks, ms)]
    return [p / d for p, d in zip(ps, dens)]


def _swa_prompt_kernel(sink_ref, q_ref, kp_ref, ko_ref, vp_ref, vo_ref, gq_ref, gk_ref,
                       o_ref, kn_ref, *, nq):
    W = WINDOW
    t_idx = lax.broadcasted_iota(jnp.int32, (W, 2 * W), 0)
    s_idx = lax.broadcasted_iota(jnp.int32, (W, 2 * W), 1)
    dist = W + t_idx - s_idx
    in_band = (dist >= 0) & (dist <= W)
    in_band_first = in_band & ((s_idx >= W) | (pl.program_id(1) > 0))
    dist_f = dist.astype(F32)
    gq = gq_ref[...]
    gk = gk_ref[...]
    units = [(j, kh) for j in range(nq) for kh in range(B_KV_HEADS)]
    qs, ks, vs, kns = [], [], [], []
    k_raw = []
    for kh in range(B_KV_HEADS):
        lo, hi = kh * B_HD, (kh + 1) * B_HD
        k_raw.append(jnp.concatenate([kp_ref[:, lo:hi], ko_ref[:, lo:hi]], axis=0))
        vs.append(jnp.concatenate([vp_ref[:, lo:hi], vo_ref[:, lo:hi]], axis=0).astype(BF16))
    for kn in _rms_all(k_raw, gk):
        kns.append(kn[nq * W:])
        ks.append(kn.astype(BF16))
    q_raw = [q_ref[j * W:(j + 1) * W, h * B_HD:(h + 1) * B_HD]
             for j, kh in units for h in range(kh * B_GROUP, (kh + 1) * B_GROUP)]
    q_n = _rms_all(q_raw, gq)
    qs = [jnp.concatenate(q_n[u * B_GROUP:(u + 1) * B_GROUP], axis=0).astype(BF16)
          for u in range(len(units))]
    scores = [_dot_nt(q, ks[kh][j * W:(j + 2) * W]) for q, (j, kh) in zip(qs, units)]
    logits, sinks = [], []
    for s_all, (j, kh) in zip(scores, units):
        valid = in_band_first if j == 0 else in_band
        for g in range(B_GROUP):
            h = kh * B_GROUP + g
            s = s_all[g * W:(g + 1) * W] * (B_HD ** -0.5)
            logits.append(jnp.where(valid, s - _alibi_slope(h) * dist_f, NEG_INF))
            sinks.append(sink_ref[h])
    pieces = [p.astype(BF16) for p in _sink_softmax_all(logits, sinks)]
    probs = [jnp.concatenate(pieces[u * B_GROUP:(u + 1) * B_GROUP], axis=0) for u in range(len(units))]
    outs = [_dot(p, vs[kh][j * W:(j + 2) * W]) for p, (j, kh) in zip(probs, units)]
    for j in range(nq):
        o_ref[j * W:(j + 1) * W, :] = jnp.concatenate(
            [outs[j * B_KV_HEADS + kh][g * W:(g + 1) * W]
             for kh in range(B_KV_HEADS) for g in range(B_GROUP)], axis=-1).astype(o_ref.dtype)
    kn_ref[0] = jnp.concatenate(kns, axis=-1)


def _swa_prompt(z, g_q, g_k, sinks, *, n_seq, seq_len, t_total, nq):
    W = WINDOW
    nb = seq_len // (nq * W)
    bq = B_HEADS * B_HD
    bkv = B_KV_HEADS * B_HD
    kcol = bq // bkv

    def own(col):
        return lambda b, i: (b * nb + i, col)

    def prev(col):
        return lambda b, i: (jnp.maximum((b * nb + i) * nq - 1, 0), col)

    return pl.pallas_call(
        functools.partial(_swa_prompt_kernel, nq=nq),
        grid=(n_seq, nb),
        in_specs=[pl.BlockSpec(memory_space=pltpu.SMEM),
                  pl.BlockSpec((nq * W, bq), own(0)),
                  pl.BlockSpec((W, bkv), prev(kcol)),
                  pl.BlockSpec((nq * W, bkv), own(kcol)),
                  pl.BlockSpec((W, bkv), prev(kcol + 1)),
                  pl.BlockSpec((nq * W, bkv), own(kcol + 1)),
                  pl.BlockSpec((1, B_HD), lambda b, i: (0, 0)),
                  pl.BlockSpec((1, B_HD), lambda b, i: (0, 0))],
        out_specs=(pl.BlockSpec((nq * W, bq), own(0)),
                   pl.BlockSpec((1, W, bkv), lambda b, i: (b, 0, 0))),
        out_shape=(jax.ShapeDtypeStruct((t_total, bq), BF16),
                   jax.ShapeDtypeStruct((n_seq, W, bkv), F32)),
        compiler_params=_params("parallel", "arbitrary"),
        name="swa_prompt",
    )(sinks, z, z, z, z, z, g_q.reshape(1, B_HD), g_k.reshape(1, B_HD))


def _swa_sample_kernel(sink_ref, q_ref, k_ref, v_ref, ck_ref, cv_ref, gq_ref, gk_ref, _,
                       o_ref, cko_ref, cvo_ref, *, n_new, n_batch):
    W = WINDOW
    T = n_new
    R = B_HEADS * T
    GT = B_GROUP * T
    r_col = lax.broadcasted_iota(jnp.int32, (R, 1), 0)
    s_idx = lax.broadcasted_iota(jnp.int32, (R, W + T), 1)
    t_col = r_col
    slope_col = jnp.full((R, 1), _alibi_slope(0), F32)
    sink_col = jnp.full((R, 1), sink_ref[0], F32)
    for h in range(1, B_HEADS):
        in_later_head = r_col >= h * T
        t_col = jnp.where(in_later_head, r_col - h * T, t_col)
        slope_col = jnp.where(in_later_head, _alibi_slope(h), slope_col)
        sink_col = jnp.where(in_later_head, sink_ref[h], sink_col)
    dist = W + t_col - s_idx
    valid = (dist >= 0) & (dist <= W)
    bias = slope_col * dist.astype(F32)
    gq = gq_ref[...]
    gk = gk_ref[...]
    qn = _rms_all([q_ref[:, h * B_HD:(h + 1) * B_HD] for h in range(B_HEADS)], gq)
    kn = _rms_all([k_ref[:, kh * B_HD:(kh + 1) * B_HD] for kh in range(B_KV_HEADS)], gk)
    qs, ks, vs = [], [], []
    for bi in range(n_batch):
        r0, r1 = bi * T, (bi + 1) * T
        for kh in range(B_KV_HEADS):
            lo, hi = kh * B_HD, (kh + 1) * B_HD
            k_all = jnp.concatenate([ck_ref[bi, :, lo:hi], kn[kh][r0:r1]], axis=0)
            v_all = jnp.concatenate([cv_ref[bi, :, lo:hi], v_ref[r0:r1, lo:hi]], axis=0)
            cko_ref[bi, :, lo:hi] = k_all[T:]
            cvo_ref[bi, :, lo:hi] = v_all[T:]
            ks.append(k_all)
            vs.append(v_all)
            heads = range(kh * B_GROUP, (kh + 1) * B_GROUP)
            qs.append(jnp.concatenate([qn[h][r0:r1] for h in heads], axis=0))
    scores = [_dot_nt(q, k) for q, k in zip(qs, ks)]
    logits = []
    for bi in range(n_batch):
        s = jnp.concatenate(scores[bi * B_KV_HEADS:(bi + 1) * B_KV_HEADS], axis=0)
        logits.append(jnp.where(valid, s * (B_HD ** -0.5) - bias, NEG_INF))
    probs = _sink_softmax_all(logits, [sink_col] * n_batch)
    outs = [_dot(probs[i // B_KV_HEADS][(i % B_KV_HEADS) * GT:(i % B_KV_HEADS + 1) * GT], v)
            for i, v in enumerate(vs)]
    rows_out = []
    for bi in range(n_batch):
        rows_out.append(jnp.concatenate(
            [outs[bi * B_KV_HEADS + kh][g * T:(g + 1) * T]
             for kh in range(B_KV_HEADS) for g in range(B_GROUP)], axis=-1))
    o_ref[...] = jnp.concatenate(rows_out, axis=0).astype(o_ref.dtype)


def _swa_sample(z, cache_k, cache_v, g_q, g_k, sinks, o_prev, *, row0, n_new, n_batch):
    W = WINDOW
    n_seq = cache_k.shape[0]
    bq = B_HEADS * B_HD
    bkv = B_KV_HEADS * B_HD
    kcol = bq // bkv
    rows = n_new * n_batch
    rb0 = row0 // rows
    ck = cache_k.reshape(n_seq, W, bkv)
    cv = cache_v.reshape(n_seq, W, bkv)
    cache_spec = pl.BlockSpec((n_batch, W, bkv), lambda i: (i, 0, 0))
    o, cko, cvo = pl.pallas_call(
        functools.partial(_swa_sample_kernel, n_new=n_new, n_batch=n_batch),
        grid=(n_seq // n_batch,),
        in_specs=[pl.BlockSpec(memory_space=pltpu.SMEM),
                  pl.BlockSpec((rows, bq), lambda i: (rb0 + i, 0)),
                  pl.BlockSpec((rows, bkv), lambda i: (rb0 + i, kcol)),
                  pl.BlockSpec((rows, bkv), lambda i: (rb0 + i, kcol + 1)),
                  cache_spec, cache_spec,
                  pl.BlockSpec((1, B_HD), lambda i: (0, 0)),
                  pl.BlockSpec((1, B_HD), lambda i: (0, 0)),
                  pl.BlockSpec(memory_space=pl.ANY)],
        out_specs=(pl.BlockSpec((rows, bq), lambda i: (rb0 + i, 0)), cache_spec, cache_spec),
        out_shape=(jax.ShapeDtypeStruct(o_prev.shape, o_prev.dtype),
                   jax.ShapeDtypeStruct(ck.shape, cache_k.dtype),
                   jax.ShapeDtypeStruct(cv.shape, cache_v.dtype)),
        input_output_aliases={8: 0},
        compiler_params=_params("parallel"),
        name="swa_sample",
    )(sinks, z, z, z, ck, cv, g_q.reshape(1, B_HD), g_k.reshape(1, B_HD), o_prev)
    return o, cko.reshape(cache_k.shape), cvo.reshape(cache_v.shape)


def _top_ranked(s, k):
    rows = s.shape[0]
    ridx = lax.broadcasted_iota(jnp.int32, s.shape, 0).astype(F32)
    rank = jnp.full(s.shape, float(k), F32)
    vals = []
    for r in range(k):
        m = jnp.max(s, axis=0, keepdims=True)
        vals.append(m)
        first = ridx == jnp.min(jnp.where(s == m, ridx, float(rows)), axis=0, keepdims=True)
        rank = jnp.where(first, float(r), rank)
        s = jnp.where(first, NEG_INF, s)
    return vals, rank


def _top_ranked_distinct(s, k):
    rank = jnp.full(s.shape, float(k), F32)
    vals = []
    for r in range(k):
        m = jnp.max(s, axis=0, keepdims=True)
        vals.append(m)
        hit = s == m
        rank = jnp.where(hit, float(r), rank)
        s = jnp.where(hit, NEG_INF, s)
    return vals, rank


def _ranked_ok(rank, k):
    n_ranked = jnp.sum(jnp.where(rank < k, 1.0, 0.0), axis=0, keepdims=True)
    return jnp.where(n_ranked == k, 1.0, 0.0)


_CAND_PAIRS = [(a, b) for a in range(P_TOPK) for b in range(P_TOPK) if (a + 1) * (b + 1) <= P_TOPK]
_CAND_ROWS = -(-len(_CAND_PAIRS) // 8) * 8


TAB_COUNT0, TAB_GATE0 = 0, 1
TAB_RANK1, TAB_GATE1 = 0, 1


def _peer_route_kernel(x_ref, g_ref, wqt_ref, keys_ref, xnt_ref, tab0_ref, tab1_ref,
                       qt_ref, cand_ref, tied_ref):
    xn = _rms(x_ref[...], g_ref[...])
    xnt = xn.T.astype(BF16)
    xnt_ref[...] = xnt
    qt_ref[...] = _dot(wqt_ref[...], xnt)
    half = keys_ref.shape[2]
    cand_ref[...] = jnp.full(cand_ref.shape, NEG_INF, F32)

    def route_head(h, top_fn):
        row0 = 2 * h * half if isinstance(h, int) else pl.multiple_of(2 * h * half, half)
        q0 = qt_ref[pl.ds(row0, half), :]
        q1 = qt_ref[pl.ds(row0 + half, half), :]
        s0 = _dot(keys_ref[2 * h], q0)
        s1 = _dot(keys_ref[2 * h + 1], q1)
        top0, rank0 = top_fn(s0, P_TOPK)
        top1, rank1 = top_fn(s1, P_TOPK)
        for r, (a, b) in enumerate(_CAND_PAIRS):
            cand_ref[r:r + 1, :] = top0[a] + top1[b]
        best, cand_rank = top_fn(cand_ref[...], P_TOPK)
        z = jnp.zeros_like(best[0])
        for c in best:
            z = z + jnp.exp(c - best[0])
        taken = jnp.where(cand_rank < P_TOPK, 1.0, 0.0)
        count0 = jnp.zeros_like(s0)
        for a in range(P_TOPK):
            rows = [r for r, (ca, _) in enumerate(_CAND_PAIRS) if ca == a]
            n_sel = jnp.sum(taken[rows[0]:rows[-1] + 1], axis=0, keepdims=True)
            count0 = jnp.where(rank0 == a, n_sel, count0)
        tab0_ref[2 * h + TAB_COUNT0] = 2.0 * count0 - 1.0
        tab0_ref[2 * h + TAB_GATE0] = jnp.exp(s0 - top0[0]) / z
        tab1_ref[2 * h + TAB_RANK1] = (2.0 * rank1).astype(BF16)
        tab1_ref[2 * h + TAB_GATE1] = jnp.exp(s1 - top1[0]).astype(BF16)
        return _ranked_ok(rank0, P_TOPK) * _ranked_ok(rank1, P_TOPK) * _ranked_ok(cand_rank, P_TOPK)

    for h in range(P_HEADS):
        ok = route_head(h, _top_ranked_distinct)
        tied_ref[h] = (jnp.min(ok) < 1.0).astype(jnp.int32)

    def redo(h, carry):
        @pl.when(tied_ref[h] != 0)
        def _():
            route_head(h, _top_ranked)
        return carry

    lax.fori_loop(0, P_HEADS, redo, 0)


def _peer_route(x, g, wqt, keys, *, tt):
    t, d = x.shape
    nq = wqt.shape[0]
    hp, nk, half = keys.shape
    tab_spec = pl.BlockSpec((2 * P_HEADS, nk, tt), lambda i: (0, 0, i))
    return pl.pallas_call(
        _peer_route_kernel,
        grid=(t // tt,),
        in_specs=[pl.BlockSpec((tt, d), lambda i: (i, 0)),
                  pl.BlockSpec((1, d), lambda i: (0, 0)),
                  pl.BlockSpec((nq, d), lambda i: (0, 0)),
                  pl.BlockSpec((hp, nk, half), lambda i: (0, 0, 0))],
        out_specs=(pl.BlockSpec((d, tt), lambda i: (0, i)), tab_spec, tab_spec),
        out_shape=(jax.ShapeDtypeStruct((d, t), BF16),
                   jax.ShapeDtypeStruct((2 * P_HEADS, nk, t), F32),
                   jax.ShapeDtypeStruct((2 * P_HEADS, nk, t), BF16)),
        scratch_shapes=[pltpu.VMEM((nq, tt), F32), pltpu.VMEM((_CAND_ROWS, tt), F32),
                        pltpu.SMEM((P_HEADS,), jnp.int32)],
        compiler_params=_params("parallel"),
        name="peer_route",
    )(x, g.reshape(1, d), wqt, keys)


def _gelu(x):
    return 0.5 * x * (1.0 + lax.erf(x * math.sqrt(0.5)))


def _rows_bf16(row, n):
    packed = jnp.broadcast_to(row, (BF16_SUBLANES, LANES)).astype(BF16)
    return jnp.concatenate([packed] * (n // BF16_SUBLANES), axis=0)


def _peer_expert_kernel(xnt_ref, tab0_ref, tab1_ref, u_ref, vt_ref, yt_ref, w0_ref, *, ne, nc):
    et = pl.program_id(1)
    n_et = pl.num_programs(1)
    tt = xnt_ref.shape[1]
    n_i = ne // N_KEYS
    assert n_i == SUBLANES and nc % LANES == 0
    n_chunk = tt // nc
    row_group = 4
    blocks_rc = [(r, c) for r in range(n_i) for c in range(nc // LANES)]

    def hidden(k):
        return _dot(u_ref[...], xnt_ref[:, k * nc:(k + 1) * nc])

    def gates(k, tile):
        rows8 = pl.ds(pl.multiple_of(tile * n_i, SUBLANES), SUBLANES)
        blocks = {}
        for c in range(nc // LANES):
            cs = slice(k * nc + c * LANES, k * nc + (c + 1) * LANES)
            for r0 in range(0, n_i, row_group):
                rows = range(r0, r0 + row_group)
                w = {r: jnp.zeros((N_KEYS, LANES), BF16) for r in rows}
                for h in range(P_HEADS):
                    rank1 = tab1_ref[2 * h + TAB_RANK1, :, cs]
                    gate1 = tab1_ref[2 * h + TAB_GATE1, :, cs]
                    count0 = tab0_ref[2 * h + TAB_COUNT0, rows8, cs]
                    gate0 = tab0_ref[2 * h + TAB_GATE0, rows8, cs]
                    for r in rows:
                        sel_gate1 = jnp.minimum(
                            jnp.maximum(_rows_bf16(count0[r:r + 1], N_KEYS) - rank1, 0.0), gate1)
                        w[r] = w[r] + sel_gate1 * _rows_bf16(gate0[r:r + 1], N_KEYS)
                for r in rows:
                    blocks[r, c] = w[r]
        return blocks

    def activate(hid, w):
        return jnp.concatenate(
            [jnp.concatenate(
                [_gelu(hid[r * N_KEYS:(r + 1) * N_KEYS, c * LANES:(c + 1) * LANES]).astype(BF16) * w[r, c]
                 for c in range(nc // LANES)], axis=1)
             for r in range(n_i)], axis=0)

    def accumulate(k, a):
        yt_ref[:, k * nc:(k + 1) * nc] += _dot(vt_ref[...], a)

    def store_gates(w):
        for r, c in blocks_rc:
            w0_ref[r * N_KEYS:(r + 1) * N_KEYS, c * LANES:(c + 1) * LANES] = w[r, c]

    @pl.when(et == 0)
    def _():
        yt_ref[...] = jnp.zeros_like(yt_ref)
        store_gates(gates(0, et))

    w = {(r, c): w0_ref[r * N_KEYS:(r + 1) * N_KEYS, c * LANES:(c + 1) * LANES] for r, c in blocks_rc}
    hid = hidden(0)
    a_prev = None
    for k in range(n_chunk):
        last = k + 1 == n_chunk
        hid_next = None if last else hidden(k + 1)
        if a_prev is not None:
            accumulate(k - 1, a_prev)
        w_next = None if last else gates(k + 1, et)
        a_prev = activate(hid, w)
        hid, w = hid_next, w_next
    w_ahead = gates(0, jnp.minimum(et + 1, n_et - 1))
    accumulate(n_chunk - 1, a_prev)
    store_gates(w_ahead)


def _peer_experts(xnt, tab0, tab1, u, vt, *, tt, ne, nc):
    d, t = xnt.shape
    n_tab, nk, _ = tab0.shape
    n_exp = u.shape[0]
    once = pl.Buffered(1)
    tab_spec = pl.BlockSpec((n_tab, nk, tt), lambda i, e: (0, 0, i), pipeline_mode=once)
    return pl.pallas_call(
        functools.partial(_peer_expert_kernel, ne=ne, nc=nc),
        grid=(t // tt, n_exp // ne),
        in_specs=[pl.BlockSpec((d, tt), lambda i, e: (0, i), pipeline_mode=once),
                  tab_spec, tab_spec,
                  pl.BlockSpec((ne, d), lambda i, e: (e, 0)),
                  pl.BlockSpec((d, ne), lambda i, e: (0, e))],
        out_specs=pl.BlockSpec((d, tt), lambda i, e: (0, i)),
        out_shape=jax.ShapeDtypeStruct((d, t), F32),
        scratch_shapes=[pltpu.VMEM((ne, nc), BF16)],
        compiler_params=_params("parallel", "arbitrary"),
        name="peer_experts",
    )(xnt, tab0, tab1, u, vt)


def _add_transposed_kernel(x_ref, yt_ref, o_ref):
    o_ref[...] = x_ref[...] + yt_ref[...].T


def _add_transposed(x, yt, *, tt, row0=0, rows=None):
    t, d = x.shape
    rows = t if rows is None else rows
    b0 = row0 // tt
    return pl.pallas_call(
        _add_transposed_kernel,
        grid=(rows // tt,),
        in_specs=[pl.BlockSpec((tt, d), lambda i: (b0 + i, 0)),
                  pl.BlockSpec((d, tt), lambda i: (0, b0 + i))],
        out_specs=pl.BlockSpec((tt, d), lambda i: (i, 0)),
        out_shape=jax.ShapeDtypeStruct((rows, d), F32),
        compiler_params=_params("parallel"),
        name="add_transposed",
    )(x, yt)


def _cast_kernel(x_ref, o_ref, *, transpose):
    x = x_ref[...]
    o_ref[...] = (x.T if transpose else x).astype(o_ref.dtype)


def _layer_cast(w, layer, dtype, *, rows, transpose):
    _, n, d = w.shape
    return pl.pallas_call(
        functools.partial(_cast_kernel, transpose=transpose),
        grid=(n // rows,),
        in_specs=[pl.BlockSpec((None, rows, d), lambda i: (layer, i, 0))],
        out_specs=(pl.BlockSpec((d, rows), lambda i: (0, i)) if transpose
                   else pl.BlockSpec((rows, d), lambda i: (i, 0))),
        out_shape=jax.ShapeDtypeStruct((d, n) if transpose else (n, d), dtype),
        compiler_params=_params("parallel"),
        name="layer_cast_t" if transpose else "layer_cast",
    )(w)


def _peer(x, g, w_q, keys, peer_u, peer_v, layer, *, tt_route, tt, ne):
    hp = keys.shape[0] * keys.shape[1]
    xnt, tab0, tab1 = _peer_route(x, g, w_q.T.astype(BF16),
                                  keys.reshape(hp, keys.shape[2], keys.shape[3]), tt=tt_route)
    u = _layer_cast(peer_u, layer, BF16, rows=512, transpose=False)
    vt = _layer_cast(peer_v, layer, BF16, rows=512, transpose=True)
    return _peer_experts(xnt, tab0, tab1, u, vt, tt=tt, ne=ne, nc=min(tt, 2 * LANES))


def _pick_tile(n, pref):
    t = pref
    while n % t:
        t //= 2
    return t


def kernel(x_prompt, x_sample, state_c, state_n, state_m, cache_k, cache_v, norm_mix, w_in_a, b_gate_a,
           norm_h_a, w_out_a, w_in_b, g_q_b, g_k_b, sink_b, w_out_b, norm_ffn, peer_wq, peer_keys,
           peer_u, peer_v):
    bp, sp, d = x_prompt.shape
    bs, ss, _ = x_sample.shape
    tp, ts = bp * sp, bs * ss
    t = tp + ts
    x = jnp.concatenate([x_prompt.reshape(tp, d), x_sample.reshape(ts, d)], axis=0)
    tm = _pick_tile(math.gcd(tp, ts), 512)
    tr = min(tm, 256)
    te = _pick_tile(t, 1024)
    n_qkvo = 2 * A_HEADS * (A_DK + A_DV)
    chunk_p = _pick_tile(sp, 256)

    w_in = w_in_a[0]
    z = _norm_matmul(x, norm_mix[0], w_in[:, :n_qkvo].astype(BF16), tm=tm, tn=1024)
    w_gate = jnp.pad(w_in[:, n_qkvo:], ((0, 0), (0, LANES - 2 * A_HEADS)))
    zg = _norm_matmul(x, norm_mix[0], w_gate, tm=tm, tn=LANES, precise=True)
    h, c_p, n_p, m_p = _mlstm(z, zg, b_gate_a[0], norm_h_a[0], row0=0, n_seq=bp, seq_len=sp,
                              chunk=chunk_p, n_sub=1, t_total=t)
    h, c_s, n_s, m_s = _mlstm(z, zg, b_gate_a[0], norm_h_a[0], row0=tp, n_seq=bs, seq_len=ss,
                              chunk=ss, n_sub=4, t_total=t,
                              state=(state_c[0], state_n[0], state_m[0]), h_prev=h)
    x = _matmul_res(h, w_out_a[0].astype(BF16), x, tm=tm, tn=1024)
    yt = _peer(x, norm_ffn[0], peer_wq[0], peer_keys[0], peer_u, peer_v, 0, tt_route=tr, tt=te, ne=1024)
    x = _add_transposed(x, yt, tt=tr)

    z = _norm_matmul(x, norm_mix[1], w_in_b[0].astype(BF16), tm=tm, tn=1024)
    o, kn_p = _swa_prompt(z, g_q_b[0], g_k_b[0], sink_b[0], n_seq=bp, seq_len=sp, t_total=t,
                          nq=2 if sp % (2 * WINDOW) == 0 else 1)
    o, ck_s, cv_s = _swa_sample(z, cache_k[0], cache_v[0], g_q_b[0], g_k_b[0], sink_b[0], o,
                                row0=tp, n_new=ss, n_batch=8)
    x = _matmul_res(o, w_out_b[0].astype(BF16), x, tm=tm, tn=1024)
    yt = _peer(x, norm_ffn[1], peer_wq[1], peer_keys[1], peer_u, peer_v, 1, tt_route=tr, tt=te, ne=1024)
    y_p = _add_transposed(x, yt, tt=tr, row0=0, rows=tp)
    y_s = _add_transposed(x, yt, tt=tr, row0=tp, rows=ts)

    bq = B_HEADS * B_HD
    bkv = B_KV_HEADS * B_HD
    v_p = z[:tp, bq + bkv:].reshape(bp, sp, bkv)[:, sp - WINDOW:]
    cache_shape = (1, bp, WINDOW, B_KV_HEADS, B_HD)
    return (y_p.reshape(bp, sp, d), y_s.reshape(bs, ss, d),
            c_p[None], n_p[None], m_p[None],
            kn_p.reshape(cache_shape), v_p.reshape(cache_shape),
            c_s[None], n_s[None], m_s[None], ck_s[None], cv_s[None])
```

```python
import functools
import math

import jax
import jax.numpy as jnp
from jax import lax
from jax.experimental import pallas as pl
from jax.experimental.pallas import tpu as pltpu

F32 = jnp.float32
BF16 = jnp.bfloat16
EPS = 1e-6
NEG_INF = float("-inf")

A_HEADS = 8
A_DK = 128
A_DV = 256
GATE_SOFTCAP = 15.0
B_HEADS = 32
B_KV_HEADS = 8
B_HD = 64
B_GROUP = B_HEADS // B_KV_HEADS
WINDOW = 128
P_HEADS = 8
N_KEYS = 128
P_TOPK = 16

LANES = 128
SUBLANES = 8
BF16_SUBLANES = 16
VMEM_LIMIT = 58 * 1024 * 1024


def _params(*sem):
    return pltpu.CompilerParams(dimension_semantics=sem, vmem_limit_bytes=VMEM_LIMIT)


def _dot(a, b):
    return jnp.dot(a, b, preferred_element_type=F32)


def _dot_nt(a, b):
    return lax.dot_general(a, b, (((1,), (1,)), ((), ())), preferred_element_type=F32)


def _dot_tn(a, b):
    return lax.dot_general(a, b, (((0,), (0,)), ((), ())), preferred_element_type=F32)


def _rms(x, g):
    return x * lax.rsqrt(jnp.mean(x * x, axis=-1, keepdims=True) + EPS) * g


def _rms_all(xs, g):
    ms = [jnp.mean(x * x, axis=-1, keepdims=True) for x in xs]
    return [x * lax.rsqrt(m + EPS) * g for x, m in zip(xs, ms)]


def _norm_matmul_kernel(x_ref, g_ref, w_ref, o_ref, xn_ref, *, precise):
    @pl.when(pl.program_id(1) == 0)
    def _():
        xn_ref[...] = _rms(x_ref[...], g_ref[...]).astype(xn_ref.dtype)

    if precise:
        o_ref[...] = jnp.dot(xn_ref[...], w_ref[...], preferred_element_type=F32,
                             precision=lax.Precision.HIGHEST)
    else:
        o_ref[...] = _dot(xn_ref[...], w_ref[...])


def _norm_matmul(x, g, w, *, tm, tn, precise=False):
    t, d = x.shape
    n = w.shape[1]
    return pl.pallas_call(
        functools.partial(_norm_matmul_kernel, precise=precise),
        grid=(t // tm, n // tn),
        in_specs=[pl.BlockSpec((tm, d), lambda i, j: (i, 0)),
                  pl.BlockSpec((1, d), lambda i, j: (0, 0)),
                  pl.BlockSpec((d, tn), lambda i, j: (0, j))],
        out_specs=pl.BlockSpec((tm, tn), lambda i, j: (i, j)),
        out_shape=jax.ShapeDtypeStruct((t, n), F32),
        scratch_shapes=[pltpu.VMEM((tm, d), F32 if precise else BF16)],
        compiler_params=_params("parallel", "arbitrary"),
        name="norm_matmul_f32" if precise else "norm_matmul",
    )(x, g.reshape(1, d), w)


def _matmul_res_kernel(a_ref, w_ref, r_ref, o_ref):
    o_ref[...] = r_ref[...] + _dot(a_ref[...], w_ref[...])


def _matmul_res(a, w, r, *, tm, tn):
    t, k = a.shape
    n = w.shape[1]
    return pl.pallas_call(
        _matmul_res_kernel,
        grid=(t // tm, n // tn),
        in_specs=[pl.BlockSpec((tm, k), lambda i, j: (i, 0)),
                  pl.BlockSpec((k, tn), lambda i, j: (0, j)),
                  pl.BlockSpec((tm, tn), lambda i, j: (i, j))],
        out_specs=pl.BlockSpec((tm, tn), lambda i, j: (i, j)),
        out_shape=jax.ShapeDtypeStruct((t, n), F32),
        compiler_params=_params("parallel", "arbitrary"),
        name="matmul_res",
    )(a, w, r)


def _log_sigmoid(x):
    return jnp.minimum(x, 0.0) - jnp.log(1.0 + jnp.exp(-jnp.abs(x)))


def _mx(x):
    return x.astype(BF16) if x.shape[0] % 16 == 0 else x


def _mlstm_kernel(*refs, chunk, n_sub, has_state):
    if has_state:
        (q_ref, k_ref, v_ref, o_ref, gc_ref, gr_ref, bc_ref, br_ref, gh_ref,
         c0_ref, n0_ref, m0_ref, _, h_ref, c_ref, n_ref, m_ref) = refs
    else:
        (q_ref, k_ref, v_ref, o_ref, gc_ref, gr_ref, bc_ref, br_ref, gh_ref,
         h_ref, c_ref, n_ref, m_ref) = refs
    L = chunk

    @pl.when(pl.program_id(1) == 0)
    def _():
        if has_state:
            c_ref[...] = c0_ref[...]
            n_ref[...] = n0_ref[...]
            m_ref[...] = m0_ref[...]
        else:
            c_ref[...] = jnp.zeros_like(c_ref)
            n_ref[...] = jnp.zeros_like(n_ref)
            m_ref[...] = jnp.zeros_like(m_ref)

    t_idx = lax.broadcasted_iota(jnp.int32, (L, L), 0)
    s_idx = lax.broadcasted_iota(jnp.int32, (L, L), 1)
    causal = s_idx <= t_idx
    heads_out = [[None] * n_sub for _ in range(A_HEADS)]
    units = [(sub, h) for sub in range(n_sub) for h in range(A_HEADS)]
    st = [dict() for _ in units]

    for u, (sub, h) in zip(st, units):
        r0, r1 = sub * L, (sub + 1) * L
        if h == 0:
            gcol = gc_ref[sub] + bc_ref[...]
            gcol = GATE_SOFTCAP * jnp.tanh(gcol / GATE_SOFTCAP)
            grow = gr_ref[sub] + br_ref[...]
            grow = GATE_SOFTCAP * jnp.tanh(grow / GATE_SOFTCAP)
            fcol_all = _log_sigmoid(gcol)
            frow_all = _log_sigmoid(grow)
        u["q"] = q_ref[r0:r1, h * A_DK:(h + 1) * A_DK]
        u["k"] = k_ref[r0:r1, h * A_DK:(h + 1) * A_DK] * (A_DK ** -0.5)
        u["vb"] = _mx(v_ref[r0:r1, h * A_DV:(h + 1) * A_DV])
        u["i_col"] = gcol[:, h:h + 1]
        u["i_row"] = grow[h:h + 1, :]
        f_col = fcol_all[:, A_HEADS + h:A_HEADS + h + 1]
        f_row = frow_all[A_HEADS + h:A_HEADS + h + 1, :]
        u["c_prev"] = c_ref[sub, h]
        u["n_prev"] = n_ref[sub, h:h + 1, :]
        u["m_prev"] = m_ref[sub, h:h + 1, 0:1]
        u["b_col"] = jnp.sum(jnp.where(causal, f_row, 0.0), axis=1, keepdims=True)
        u["b_row"] = jnp.sum(jnp.where(t_idx <= s_idx, f_col, 0.0), axis=0, keepdims=True)
        qb = _mx(u["q"])
        u["qk"] = _dot_nt(qb, _mx(u["k"]))
        u["qc"] = _dot(qb, u["c_prev"].astype(qb.dtype))
        u["qn"] = jnp.sum(u["q"] * u["n_prev"], axis=1, keepdims=True)

    for u in st:
        b_end = u["b_col"][L - 1:L, :]
        u["log_intra"] = jnp.where(causal, u["b_col"] - u["b_row"] + u["i_row"], NEG_INF)
        u["log_inter"] = u["b_col"] + u["m_prev"]
        u["max_intra"] = jnp.max(u["log_intra"], axis=1, keepdims=True)
        u["log_w_col"] = b_end - u["b_col"] + u["i_col"]
        u["m_end"] = b_end + u["m_prev"]
        u["max_w"] = jnp.max(b_end - u["b_row"] + u["i_row"], axis=1, keepdims=True)

    for u in st:
        u["m_t"] = jnp.maximum(u["log_inter"], u["max_intra"])
        u["w_inter"] = jnp.exp(u["log_inter"] - u["m_t"])
        u["s"] = u["qk"] * jnp.exp(u["log_intra"] - u["m_t"])
        u["s_sum"] = jnp.sum(u["s"], axis=1, keepdims=True)
        u["m_new"] = jnp.maximum(u["m_end"], u["max_w"])
        u["decay"] = jnp.exp(u["m_end"] - u["m_new"])
        kw = u["k"] * jnp.exp(u["log_w_col"] - u["m_new"])
        u["kw_sum"] = jnp.sum(kw, axis=0, keepdims=True)
        u["sv"] = _dot(_mx(u["s"]), u["vb"])
        u["kv"] = _dot_tn(_mx(kw), u["vb"])

    for u, (sub, h) in zip(st, units):
        num = u["w_inter"] * u["qc"] + u["sv"]
        den = u["w_inter"] * u["qn"] + u["s_sum"]
        u["hh"] = num / jnp.maximum(jnp.abs(den), jnp.exp(-u["m_t"]))
        u["hh_ms"] = jnp.mean(u["hh"] * u["hh"], axis=-1, keepdims=True)
        c_ref[sub, h] = u["decay"] * u["c_prev"] + u["kv"]
        n_ref[sub, h:h + 1, :] = u["decay"] * u["n_prev"] + u["kw_sum"]
        m_ref[sub, h:h + 1, :] = jnp.broadcast_to(u["m_new"], (1, LANES))

    for u, (sub, h) in zip(st, units):
        r0, r1 = sub * L, (sub + 1) * L
        hn = u["hh"] * lax.rsqrt(u["hh_ms"] + EPS) * gh_ref[:, h * A_DV:(h + 1) * A_DV]
        heads_out[h][sub] = hn * jax.nn.sigmoid(o_ref[r0:r1, h * A_DV:(h + 1) * A_DV])

    for h in range(A_HEADS):
        rows = heads_out[h][0] if n_sub == 1 else jnp.concatenate(heads_out[h], axis=0)
        h_ref[:, h * A_DV:(h + 1) * A_DV] = rows.astype(h_ref.dtype)


def _mlstm(z, zg, b_gate, g_h, *, row0, n_seq, seq_len, chunk, n_sub, t_total, state=None, h_prev=None):
    L = chunk
    nc = seq_len // L
    assert n_sub == 1 or nc == 1
    hd = A_HEADS * A_DK
    hv = A_HEADS * A_DV
    R = n_sub * L
    rb0 = row0 // R
    rows = n_seq * seq_len
    gates = zg[row0:row0 + rows, :2 * A_HEADS].reshape(n_seq * nc, L, 2 * A_HEADS)
    gates_t = jnp.swapaxes(gates, 1, 2)
    has_state = state is not None

    def rmap(col):
        return lambda s, c: (rb0 + s * nc + c, col)

    in_specs = [pl.BlockSpec((R, hd), rmap(0)),
                pl.BlockSpec((R, hd), rmap(1)),
                pl.BlockSpec((R, hv), rmap(hd * 2 // hv)),
                pl.BlockSpec((R, hv), rmap(hd * 2 // hv + 1)),
                pl.BlockSpec((n_sub, L, 2 * A_HEADS), lambda s, c: (s * nc + c, 0, 0)),
                pl.BlockSpec((n_sub, 2 * A_HEADS, L), lambda s, c: (s * nc + c, 0, 0)),
                pl.BlockSpec((1, 2 * A_HEADS), lambda s, c: (0, 0)),
                pl.BlockSpec((2 * A_HEADS, 1), lambda s, c: (0, 0)),
                pl.BlockSpec((1, hv), lambda s, c: (0, 0))]
    args = [z, z, z, z, gates, gates_t, b_gate.reshape(1, -1), b_gate.reshape(-1, 1), g_h.reshape(1, hv)]
    state_specs = [pl.BlockSpec((n_sub, A_HEADS, A_DK, A_DV), lambda s, c: (s, 0, 0, 0)),
                   pl.BlockSpec((n_sub, A_HEADS, A_DK), lambda s, c: (s, 0, 0)),
                   pl.BlockSpec((n_sub, A_HEADS, LANES), lambda s, c: (s, 0, 0))]
    aliases = {}
    if has_state:
        c0, n0, m0 = state
        in_specs += state_specs
        args += [c0, n0, jnp.broadcast_to(m0[..., None], m0.shape + (LANES,))]
        in_specs.append(pl.BlockSpec(memory_space=pl.ANY))
        args.append(h_prev)
        aliases = {len(args) - 1: 0}
    out_shape = (jax.ShapeDtypeStruct((t_total, hv), BF16),
                 jax.ShapeDtypeStruct((n_seq, A_HEADS, A_DK, A_DV), F32),
                 jax.ShapeDtypeStruct((n_seq, A_HEADS, A_DK), F32),
                 jax.ShapeDtypeStruct((n_seq, A_HEADS, LANES), F32))
    h, c, n, m = pl.pallas_call(
        functools.partial(_mlstm_kernel, chunk=L, n_sub=n_sub, has_state=has_state),
        grid=(n_seq // n_sub, nc),
        in_specs=in_specs,
        out_specs=(pl.BlockSpec((R, hv), rmap(0)),) + tuple(state_specs),
        out_shape=out_shape,
        input_output_aliases=aliases,
        compiler_params=_params("parallel", "arbitrary"),
        name="mlstm_state" if has_state else "mlstm",
    )(*args)
    return h, c, n, m[..., 0]


def _alibi_slope(h):
    return 2.0 ** (-8.0 * (h + 1.0) / B_HEADS)


def _sink_softmax_all(logits, sinks):
    ms = [jnp.maximum(jnp.max(l, axis=-1, keepdims=True), s) for l, s in zip(logits, sinks)]
    ps = [jnp.exp(l - m) for l, m in zip(logits, ms)]
    dens = [jnp.sum(p, axis=-1, keepdims=True) + jnp.exp(s - m) for p, s, m in zip(ps, sinks, ms)]
    return [p / d for p, d in zip(ps, dens)]


def _swa_prompt_kernel(sink_ref, q_ref, kp_ref, ko_ref, vp_ref, vo_ref, gq_ref, gk_ref,
                       o_ref, kn_ref, *, nq):
    W = WINDOW
    t_idx = lax.broadcasted_iota(jnp.int32, (W, 2 * W), 0)
    s_idx = lax.broadcasted_iota(jnp.int32, (W, 2 * W), 1)
    dist = W + t_idx - s_idx
    in_band = (dist >= 0) & (dist <= W)
    in_band_first = in_band & ((s_idx >= W) | (pl.program_id(1) > 0))
    dist_f = dist.astype(F32)
    gq = gq_ref[...]
    gk = gk_ref[...]
    units = [(j, kh) for j in range(nq) for kh in range(B_KV_HEADS)]
    qs, ks, vs, kns = [], [], [], []
    k_raw = []
    for kh in range(B_KV_HEADS):
        lo, hi = kh * B_HD, (kh + 1) * B_HD
        k_raw.append(jnp.concatenate([kp_ref[:, lo:hi], ko_ref[:, lo:hi]], axis=0))
        vs.append(jnp.concatenate([vp_ref[:, lo:hi], vo_ref[:, lo:hi]], axis=0).astype(BF16))
    for kn in _rms_all(k_raw, gk):
        kns.append(kn[nq * W:])
        ks.append(kn.astype(BF16))
    q_raw = [q_ref[j * W:(j + 1) * W, h * B_HD:(h + 1) * B_HD]
             for j, kh in units for h in range(kh * B_GROUP, (kh + 1) * B_GROUP)]
    q_n = _rms_all(q_raw, gq)
    qs = [jnp.concatenate(q_n[u * B_GROUP:(u + 1) * B_GROUP], axis=0).astype(BF16)
          for u in range(len(units))]
    scores = [_dot_nt(q, ks[kh][j * W:(j + 2) * W]) for q, (j, kh) in zip(qs, units)]
    logits, sinks = [], []
    for s_all, (j, kh) in zip(scores, units):
        valid = in_band_first if j == 0 else in_band
        for g in range(B_GROUP):
            h = kh * B_GROUP + g
            s = s_all[g * W:(g + 1) * W] * (B_HD ** -0.5)
            logits.append(jnp.where(valid, s - _alibi_slope(h) * dist_f, NEG_INF))
            sinks.append(sink_ref[h])
    pieces = [p.astype(BF16) for p in _sink_softmax_all(logits, sinks)]
    probs = [jnp.concatenate(pieces[u * B_GROUP:(u + 1) * B_GROUP], axis=0) for u in range(len(units))]
    outs = [_dot(p, vs[kh][j * W:(j + 2) * W]) for p, (j, kh) in zip(probs, units)]
    for j in range(nq):
        o_ref[j * W:(j + 1) * W, :] = jnp.concatenate(
            [outs[j * B_KV_HEADS + kh][g * W:(g + 1) * W]
             for kh in range(B_KV_HEADS) for g in range(B_GROUP)], axis=-1).astype(o_ref.dtype)
    kn_ref[0] = jnp.concatenate(kns, axis=-1)


def _swa_prompt(z, g_q, g_k, sinks, *, n_seq, seq_len, t_total, nq):
    W = WINDOW
    nb = seq_len // (nq * W)
    bq = B_HEADS * B_HD
    bkv = B_KV_HEADS * B_HD
    kcol = bq // bkv

    def own(col):
        return lambda b, i: (b * nb + i, col)

    def prev(col):
        return lambda b, i: (jnp.maximum((b * nb + i) * nq - 1, 0), col)

    return pl.pallas_call(
        functools.partial(_swa_prompt_kernel, nq=nq),
        grid=(n_seq, nb),
        in_specs=[pl.BlockSpec(memory_space=pltpu.SMEM),
                  pl.BlockSpec((nq * W, bq), own(0)),
                  pl.BlockSpec((W, bkv), prev(kcol)),
                  pl.BlockSpec((nq * W, bkv), own(kcol)),
                  pl.BlockSpec((W, bkv), prev(kcol + 1)),
                  pl.BlockSpec((nq * W, bkv), own(kcol + 1)),
                  pl.BlockSpec((1, B_HD), lambda b, i: (0, 0)),
                  pl.BlockSpec((1, B_HD), lambda b, i: (0, 0))],
        out_specs=(pl.BlockSpec((nq * W, bq), own(0)),
                   pl.BlockSpec((1, W, bkv), lambda b, i: (b, 0, 0))),
        out_shape=(jax.ShapeDtypeStruct((t_total, bq), BF16),
                   jax.ShapeDtypeStruct((n_seq, W, bkv), F32)),
        compiler_params=_params("parallel", "arbitrary"),
        name="swa_prompt",
    )(sinks, z, z, z, z, z, g_q.reshape(1, B_HD), g_k.reshape(1, B_HD))


def _swa_sample_kernel(sink_ref, q_ref, k_ref, v_ref, ck_ref, cv_ref, gq_ref, gk_ref, _,
                       o_ref, cko_ref, cvo_ref, *, n_new, n_batch):
    W = WINDOW
    T = n_new
    R = B_HEADS * T
    GT = B_GROUP * T
    r_col = lax.broadcasted_iota(jnp.int32, (R, 1), 0)
    s_idx = lax.broadcasted_iota(jnp.int32, (R, W + T), 1)
    t_col = r_col
    slope_col = jnp.full((R, 1), _alibi_slope(0), F32)
    sink_col = jnp.full((R, 1), sink_ref[0], F32)
    for h in range(1, B_HEADS):
        in_later_head = r_col >= h * T
        t_col = jnp.where(in_later_head, r_col - h * T, t_col)
        slope_col = jnp.where(in_later_head, _alibi_slope(h), slope_col)
        sink_col = jnp.where(in_later_head, sink_ref[h], sink_col)
    dist = W + t_col - s_idx
    valid = (dist >= 0) & (dist <= W)
    bias = slope_col * dist.astype(F32)
    gq = gq_ref[...]
    gk = gk_ref[...]
    qn = _rms_all([q_ref[:, h * B_HD:(h + 1) * B_HD] for h in range(B_HEADS)], gq)
    kn = _rms_all([k_ref[:, kh * B_HD:(kh + 1) * B_HD] for kh in range(B_KV_HEADS)], gk)
    qs, ks, vs = [], [], []
    for bi in range(n_batch):
        r0, r1 = bi * T, (bi + 1) * T
        for kh in range(B_KV_HEADS):
            lo, hi = kh * B_HD, (kh + 1) * B_HD
            k_all = jnp.concatenate([ck_ref[bi, :, lo:hi], kn[kh][r0:r1]], axis=0)
            v_all = jnp.concatenate([cv_ref[bi, :, lo:hi], v_ref[r0:r1, lo:hi]], axis=0)
            cko_ref[bi, :, lo:hi] = k_all[T:]
            cvo_ref[bi, :, lo:hi] = v_all[T:]
            ks.append(k_all)
            vs.append(v_all)
            heads = range(kh * B_GROUP, (kh + 1) * B_GROUP)
            qs.append(jnp.concatenate([qn[h][r0:r1] for h in heads], axis=0))
    scores = [_dot_nt(q, k) for q, k in zip(qs, ks)]
    logits = []
    for bi in range(n_batch):
        s = jnp.concatenate(scores[bi * B_KV_HEADS:(bi + 1) * B_KV_HEADS], axis=0)
        logits.append(jnp.where(valid, s * (B_HD ** -0.5) - bias, NEG_INF))
    probs = _sink_softmax_all(logits, [sink_col] * n_batch)
    outs = [_dot(probs[i // B_KV_HEADS][(i % B_KV_HEADS) * GT:(i % B_KV_HEADS + 1) * GT], v)
            for i, v in enumerate(vs)]
    rows_out = []
    for bi in range(n_batch):
        rows_out.append(jnp.concatenate(
            [outs[bi * B_KV_HEADS + kh][g * T:(g + 1) * T]
             for kh in range(B_KV_HEADS) for g in range(B_GROUP)], axis=-1))
    o_ref[...] = jnp.concatenate(rows_out, axis=0).astype(o_ref.dtype)


def _swa_sample(z, cache_k, cache_v, g_q, g_k, sinks, o_prev, *, row0, n_new, n_batch):
    W = WINDOW
    n_seq = cache_k.shape[0]
    bq = B_HEADS * B_HD
    bkv = B_KV_HEADS * B_HD
    kcol = bq // bkv
    rows = n_new * n_batch
    rb0 = row0 // rows
    ck = cache_k.reshape(n_seq, W, bkv)
    cv = cache_v.reshape(n_seq, W, bkv)
    cache_spec = pl.BlockSpec((n_batch, W, bkv), lambda i: (i, 0, 0))
    o, cko, cvo = pl.pallas_call(
        functools.partial(_swa_sample_kernel, n_new=n_new, n_batch=n_batch),
        grid=(n_seq // n_batch,),
        in_specs=[pl.BlockSpec(memory_space=pltpu.SMEM),
                  pl.BlockSpec((rows, bq), lambda i: (rb0 + i, 0)),
                  pl.BlockSpec((rows, bkv), lambda i: (rb0 + i, kcol)),
                  pl.BlockSpec((rows, bkv), lambda i: (rb0 + i, kcol + 1)),
                  cache_spec, cache_spec,
                  pl.BlockSpec((1, B_HD), lambda i: (0, 0)),
                  pl.BlockSpec((1, B_HD), lambda i: (0, 0)),
                  pl.BlockSpec(memory_space=pl.ANY)],
        out_specs=(pl.BlockSpec((rows, bq), lambda i: (rb0 + i, 0)), cache_spec, cache_spec),
        out_shape=(jax.ShapeDtypeStruct(o_prev.shape, o_prev.dtype),
                   jax.ShapeDtypeStruct(ck.shape, cache_k.dtype),
                   jax.ShapeDtypeStruct(cv.shape, cache_v.dtype)),
        input_output_aliases={8: 0},
        compiler_params=_params("parallel"),
        name="swa_sample",
    )(sinks, z, z, z, ck, cv, g_q.reshape(1, B_HD), g_k.reshape(1, B_HD), o_prev)
    return o, cko.reshape(cache_k.shape), cvo.reshape(cache_v.shape)


def _top_ranked(s, k):
    rows = s.shape[0]
    ridx = lax.broadcasted_iota(jnp.int32, s.shape, 0).astype(F32)
    rank = jnp.full(s.shape, float(k), F32)
    vals = []
    for r in range(k):
        m = jnp.max(s, axis=0, keepdims=True)
        vals.append(m)
        first = ridx == jnp.min(jnp.where(s == m, ridx, float(rows)), axis=0, keepdims=True)
        rank = jnp.where(first, float(r), rank)
        s = jnp.where(first, NEG_INF, s)
    return vals, rank


def _top_ranked_distinct(s, k):
    rank = jnp.full(s.shape, float(k), F32)
    vals = []
    for r in range(k):
        m = jnp.max(s, axis=0, keepdims=True)
        vals.append(m)
        hit = s == m
        rank = jnp.where(hit, float(r), rank)
        s = jnp.where(hit, NEG_INF, s)
    return vals, rank


def _ranked_ok(rank, k):
    n_ranked = jnp.sum(jnp.where(rank < k, 1.0, 0.0), axis=0, keepdims=True)
    return jnp.where(n_ranked == k, 1.0, 0.0)


_CAND_PAIRS = [(a, b) for a in range(P_TOPK) for b in range(P_TOPK) if (a + 1) * (b + 1) <= P_TOPK]
_CAND_ROWS = -(-len(_CAND_PAIRS) // 8) * 8


TAB_COUNT0, TAB_GATE0 = 0, 1
TAB_RANK1, TAB_GATE1 = 0, 1


def _peer_route_kernel(x_ref, g_ref, wqt_ref, keys_ref, xnt_ref, tab0_ref, tab1_ref,
                       qt_ref, cand_ref, tied_ref):
    xn = _rms(x_ref[...], g_ref[...])
    xnt = xn.T.astype(BF16)
    xnt_ref[...] = xnt
    qt_ref[...] = _dot(wqt_ref[...], xnt)
    half = keys_ref.shape[2]
    cand_ref[...] = jnp.full(cand_ref.shape, NEG_INF, F32)

    def route_head(h, top_fn):
        row0 = 2 * h * half if isinstance(h, int) else pl.multiple_of(2 * h * half, half)
        q0 = qt_ref[pl.ds(row0, half), :]
        q1 = qt_ref[pl.ds(row0 + half, half), :]
        s0 = _dot(keys_ref[2 * h], q0)
        s1 = _dot(keys_ref[2 * h + 1], q1)
        top0, rank0 = top_fn(s0, P_TOPK)
        top1, rank1 = top_fn(s1, P_TOPK)
        for r, (a, b) in enumerate(_CAND_PAIRS):
            cand_ref[r:r + 1, :] = top0[a] + top1[b]
        best, cand_rank = top_fn(cand_ref[...], P_TOPK)
        z = jnp.zeros_like(best[0])
        for c in best:
            z = z + jnp.exp(c - best[0])
        taken = jnp.where(cand_rank < P_TOPK, 1.0, 0.0)
        count0 = jnp.zeros_like(s0)
        for a in range(P_TOPK):
            rows = [r for r, (ca, _) in enumerate(_CAND_PAIRS) if ca == a]
            n_sel = jnp.sum(taken[rows[0]:rows[-1] + 1], axis=0, keepdims=True)
            count0 = jnp.where(rank0 == a, n_sel, count0)
        tab0_ref[2 * h + TAB_COUNT0] = 2.0 * count0 - 1.0
        tab0_ref[2 * h + TAB_GATE0] = jnp.exp(s0 - top0[0]) / z
        tab1_ref[2 * h + TAB_RANK1] = (2.0 * rank1).astype(BF16)
        tab1_ref[2 * h + TAB_GATE1] = jnp.exp(s1 - top1[0]).astype(BF16)
        return _ranked_ok(rank0, P_TOPK) * _ranked_ok(rank1, P_TOPK) * _ranked_ok(cand_rank, P_TOPK)

    for h in range(P_HEADS):
        ok = route_head(h, _top_ranked_distinct)
        tied_ref[h] = (jnp.min(ok) < 1.0).astype(jnp.int32)

    def redo(h, carry):
        @pl.when(tied_ref[h] != 0)
        def _():
            route_head(h, _top_ranked)
        return carry

    lax.fori_loop(0, P_HEADS, redo, 0)


def _peer_route(x, g, wqt, keys, *, tt):
    t, d = x.shape
    nq = wqt.shape[0]
    hp, nk, half = keys.shape
    tab_spec = pl.BlockSpec((2 * P_HEADS, nk, tt), lambda i: (0, 0, i))
    return pl.pallas_call(
        _peer_route_kernel,
        grid=(t // tt,),
        in_specs=[pl.BlockSpec((tt, d), lambda i: (i, 0)),
                  pl.BlockSpec((1, d), lambda i: (0, 0)),
                  pl.BlockSpec((nq, d), lambda i: (0, 0)),
                  pl.BlockSpec((hp, nk, half), lambda i: (0, 0, 0))],
        out_specs=(pl.BlockSpec((d, tt), lambda i: (0, i)), tab_spec, tab_spec),
        out_shape=(jax.ShapeDtypeStruct((d, t), BF16),
                   jax.ShapeDtypeStruct((2 * P_HEADS, nk, t), F32),
                   jax.ShapeDtypeStruct((2 * P_HEADS, nk, t), BF16)),
        scratch_shapes=[pltpu.VMEM((nq, tt), F32), pltpu.VMEM((_CAND_ROWS, tt), F32),
                        pltpu.SMEM((P_HEADS,), jnp.int32)],
        compiler_params=_params("parallel"),
        name="peer_route",
    )(x, g.reshape(1, d), wqt, keys)


def _gelu(x):
    return 0.5 * x * (1.0 + lax.erf(x * math.sqrt(0.5)))


def _rows_bf16(row, n):
    packed = jnp.broadcast_to(row, (BF16_SUBLANES, LANES)).astype(BF16)
    return jnp.concatenate([packed] * (n // BF16_SUBLANES), axis=0)


def _peer_expert_kernel(xnt_ref, tab0_ref, tab1_ref, u_ref, vt_ref, yt_ref, w0_ref, *, ne, nc):
    et = pl.program_id(1)
    n_et = pl.num_programs(1)
    tt = xnt_ref.shape[1]
    n_i = ne // N_KEYS
    assert n_i == SUBLANES and nc % LANES == 0
    n_chunk = tt // nc
    row_group = 4
    blocks_rc = [(r, c) for r in range(n_i) for c in range(nc // LANES)]

    def hidden(k):
        return _dot(u_ref[...], xnt_ref[:, k * nc:(k + 1) * nc])

    def gates(k, tile):
        rows8 = pl.ds(pl.multiple_of(tile * n_i, SUBLANES), SUBLANES)
        blocks = {}
        for c in range(nc // LANES):
            cs = slice(k * nc + c * LANES, k * nc + (c + 1) * LANES)
            for r0 in range(0, n_i, row_group):
                rows = range(r0, r0 + row_group)
                w = {r: jnp.zeros((N_KEYS, LANES), BF16) for r in rows}
                for h in range(P_HEADS):
                    rank1 = tab1_ref[2 * h + TAB_RANK1, :, cs]
                    gate1 = tab1_ref[2 * h + TAB_GATE1, :, cs]
                    count0 = tab0_ref[2 * h + TAB_COUNT0, rows8, cs]
                    gate0 = tab0_ref[2 * h + TAB_GATE0, rows8, cs]
                    for r in rows:
                        sel_gate1 = jnp.minimum(
                            jnp.maximum(_rows_bf16(count0[r:r + 1], N_KEYS) - rank1, 0.0), gate1)
                        w[r] = w[r] + sel_gate1 * _rows_bf16(gate0[r:r + 1], N_KEYS)
                for r in rows:
                    blocks[r, c] = w[r]
        return blocks

    def activate(hid, w):
        return jnp.concatenate(
            [jnp.concatenate(
                [_gelu(hid[r * N_KEYS:(r + 1) * N_KEYS, c * LANES:(c + 1) * LANES]).astype(BF16) * w[r, c]
                 for c in range(nc // LANES)], axis=1)
             for r in range(n_i)], axis=0)

    def accumulate(k, a):
        yt_ref[:, k * nc:(k + 1) * nc] += _dot(vt_ref[...], a)

    def store_gates(w):
        for r, c in blocks_rc:
            w0_ref[r * N_KEYS:(r + 1) * N_KEYS, c * LANES:(c + 1) * LANES] = w[r, c]

    @pl.when(et == 0)
    def _():
        yt_ref[...] = jnp.zeros_like(yt_ref)
        store_gates(gates(0, et))

    w = {(r, c): w0_ref[r * N_KEYS:(r + 1) * N_KEYS, c * LANES:(c + 1) * LANES] for r, c in blocks_rc}
    hid = hidden(0)
    a_prev = None
    for k in range(n_chunk):
        last = k + 1 == n_chunk
        hid_next = None if last else hidden(k + 1)
        if a_prev is not None:
            accumulate(k - 1, a_prev)
        w_next = None if last else gates(k + 1, et)
        a_prev = activate(hid, w)
        hid, w = hid_next, w_next
    w_ahead = gates(0, jnp.minimum(et + 1, n_et - 1))
    accumulate(n_chunk - 1, a_prev)
    store_gates(w_ahead)


def _peer_experts(xnt, tab0, tab1, u, vt, *, tt, ne, nc):
    d, t = xnt.shape
    n_tab, nk, _ = tab0.shape
    n_exp = u.shape[0]
    once = pl.Buffered(1)
    tab_spec = pl.BlockSpec((n_tab, nk, tt), lambda i, e: (0, 0, i), pipeline_mode=once)
    return pl.pallas_call(
        functools.partial(_peer_expert_kernel, ne=ne, nc=nc),
        grid=(t // tt, n_exp // ne),
        in_specs=[pl.BlockSpec((d, tt), lambda i, e: (0, i), pipeline_mode=once),
                  tab_spec, tab_spec,
                  pl.BlockSpec((ne, d), lambda i, e: (e, 0)),
                  pl.BlockSpec((d, ne), lambda i, e: (0, e))],
        out_specs=pl.BlockSpec((d, tt), lambda i, e: (0, i), pipeline_mode=once),
        out_shape=jax.ShapeDtypeStruct((d, t), F32),
        scratch_shapes=[pltpu.VMEM((ne, nc), BF16)],
        compiler_params=_params("parallel", "arbitrary"),
        name="peer_experts",
    )(xnt, tab0, tab1, u, vt)


def _add_transposed_kernel(x_ref, yt_ref, o_ref):
    o_ref[...] = x_ref[...] + yt_ref[...].T


def _add_transposed(x, yt, *, tt, row0=0, rows=None):
    t, d = x.shape
    rows = t if rows is None else rows
    b0 = row0 // tt
    return pl.pallas_call(
        _add_transposed_kernel,
        grid=(rows // tt,),
        in_specs=[pl.BlockSpec((tt, d), lambda i: (b0 + i, 0)),
                  pl.BlockSpec((d, tt), lambda i: (0, b0 + i))],
        out_specs=pl.BlockSpec((tt, d), lambda i: (i, 0)),
        out_shape=jax.ShapeDtypeStruct((rows, d), F32),
        compiler_params=_params("parallel"),
        name="add_transposed",
    )(x, yt)


def _cast_kernel(x_ref, o_ref, *, transpose):
    x = x_ref[...]
    o_ref[...] = (x.T if transpose else x).astype(o_ref.dtype)


def _layer_cast(w, layer, dtype, *, rows, transpose):
    _, n, d = w.shape
    return pl.pallas_call(
        functools.partial(_cast_kernel, transpose=transpose),
        grid=(n // rows,),
        in_specs=[pl.BlockSpec((None, rows, d), lambda i: (layer, i, 0))],
        out_specs=(pl.BlockSpec((d, rows), lambda i: (0, i)) if transpose
                   else pl.BlockSpec((rows, d), lambda i: (i, 0))),
        out_shape=jax.ShapeDtypeStruct((d, n) if transpose else (n, d), dtype),
        compiler_params=_params("parallel"),
        name="layer_cast_t" if transpose else "layer_cast",
    )(w)


def _peer(x, g, peer_wq, keys, peer_u, peer_v, layer, *, tt_route, tt, ne):
    hp = keys.shape[0] * keys.shape[1]
    wqt = _layer_cast(peer_wq, layer, BF16, rows=512, transpose=True)
    xnt, tab0, tab1 = _peer_route(x, g, wqt, keys.reshape(hp, keys.shape[2], keys.shape[3]), tt=tt_route)
    u = _layer_cast(peer_u, layer, BF16, rows=512, transpose=False)
    vt = _layer_cast(peer_v, layer, BF16, rows=512, transpose=True)
    return _peer_experts(xnt, tab0, tab1, u, vt, tt=tt, ne=ne, nc=min(tt, 4 * LANES))


def _pick_tile(n, pref):
    t = pref
    while n % t:
        t //= 2
    return t


def kernel(x_prompt, x_sample, state_c, state_n, state_m, cache_k, cache_v, norm_mix, w_in_a, b_gate_a,
           norm_h_a, w_out_a, w_in_b, g_q_b, g_k_b, sink_b, w_out_b, norm_ffn, peer_wq, peer_keys,
           peer_u, peer_v):
    bp, sp, d = x_prompt.shape
    bs, ss, _ = x_sample.shape
    tp, ts = bp * sp, bs * ss
    t = tp + ts
    x = jnp.concatenate([x_prompt.reshape(tp, d), x_sample.reshape(ts, d)], axis=0)
    tm = _pick_tile(math.gcd(tp, ts), 1024)
    tr = min(tm, 256)
    te = _pick_tile(t, 1024)
    n_qkvo = 2 * A_HEADS * (A_DK + A_DV)
    chunk_p = _pick_tile(sp, 256)

    w_in = w_in_a[0]
    z = _norm_matmul(x, norm_mix[0], w_in[:, :n_qkvo].astype(BF16), tm=tm, tn=1024)
    w_gate = jnp.pad(w_in[:, n_qkvo:], ((0, 0), (0, LANES - 2 * A_HEADS)))
    zg = _norm_matmul(x, norm_mix[0], w_gate, tm=tm, tn=LANES, precise=True)
    h, c_p, n_p, m_p = _mlstm(z, zg, b_gate_a[0], norm_h_a[0], row0=0, n_seq=bp, seq_len=sp,
                              chunk=chunk_p, n_sub=1, t_total=t)
    h, c_s, n_s, m_s = _mlstm(z, zg, b_gate_a[0], norm_h_a[0], row0=tp, n_seq=bs, seq_len=ss,
                              chunk=ss, n_sub=4, t_total=t,
                              state=(state_c[0], state_n[0], state_m[0]), h_prev=h)
    x = _matmul_res(h, w_out_a[0].astype(BF16), x, tm=tm, tn=1024)
    yt = _peer(x, norm_ffn[0], peer_wq, peer_keys[0], peer_u, peer_v, 0, tt_route=tr, tt=te, ne=1024)
    x = _add_transposed(x, yt, tt=tr)

    z = _norm_matmul(x, norm_mix[1], w_in_b[0].astype(BF16), tm=tm, tn=1024)
    o, kn_p = _swa_prompt(z, g_q_b[0], g_k_b[0], sink_b[0], n_seq=bp, seq_len=sp, t_total=t,
                          nq=2 if sp % (2 * WINDOW) == 0 else 1)
    o, ck_s, cv_s = _swa_sample(z, cache_k[0], cache_v[0], g_q_b[0], g_k_b[0], sink_b[0], o,
                                row0=tp, n_new=ss, n_batch=8)
    x = _matmul_res(o, w_out_b[0].astype(BF16), x, tm=tm, tn=1024)
    yt = _peer(x, norm_ffn[1], peer_wq, peer_keys[1], peer_u, peer_v, 1, tt_route=tr, tt=te, ne=1024)
    y_p = _add_transposed(x, yt, tt=tr, row0=0, rows=tp)
    y_s = _add_transposed(x, yt, tt=tr, row0=tp, rows=ts)

    bq = B_HEADS * B_HD
    bkv = B_KV_HEADS * B_HD
    v_p = z[:tp, bq + bkv:].reshape(bp, sp, bkv)[:, sp - WINDOW:]
    cache_shape = (1, bp, WINDOW, B_KV_HEADS, B_HD)
    return (y_p.reshape(bp, sp, d), y_s.reshape(bs, ss, d),
            c_p[None], n_p[None], m_p[None],
            kn_p.reshape(cache_shape), v_p.reshape(cache_shape),
            c_s[None], n_s[None], m_s[None], ck_s[None], cv_s[None])
```

```python
import functools
import math

import jax
import jax.numpy as jnp
from jax import lax
from jax.experimental import pallas as pl
from jax.experimental.pallas import tpu as pltpu

F32 = jnp.float32
BF16 = jnp.bfloat16
EPS = 1e-6
NEG_INF = float("-inf")

A_HEADS = 8
A_DK = 128
A_DV = 256
GATE_SOFTCAP = 15.0
B_HEADS = 32
B_KV_HEADS = 8
B_HD = 64
B_GROUP = B_HEADS // B_KV_HEADS
WINDOW = 128
P_HEADS = 8
N_KEYS = 128
P_TOPK = 16

LANES = 128
SUBLANES = 8
BF16_SUBLANES = 16
VMEM_LIMIT = 58 * 1024 * 1024


def _params(*sem):
    return pltpu.CompilerParams(dimension_semantics=sem, vmem_limit_bytes=VMEM_LIMIT)


def _dot(a, b):
    return jnp.dot(a, b, preferred_element_type=F32)


def _dot_nt(a, b):
    return lax.dot_general(a, b, (((1,), (1,)), ((), ())), preferred_element_type=F32)


def _dot_tn(a, b):
    return lax.dot_general(a, b, (((0,), (0,)), ((), ())), preferred_element_type=F32)


def _rms(x, g):
    return x * lax.rsqrt(jnp.mean(x * x, axis=-1, keepdims=True) + EPS) * g


def _rms_all(xs, g):
    ms = [jnp.mean(x * x, axis=-1, keepdims=True) for x in xs]
    return [x * lax.rsqrt(m + EPS) * g for x, m in zip(xs, ms)]


def _norm_matmul_kernel(x_ref, g_ref, w_ref, o_ref, xn_ref, *, precise):
    @pl.when(pl.program_id(1) == 0)
    def _():
        xn_ref[...] = _rms(x_ref[...], g_ref[...]).astype(xn_ref.dtype)

    if precise:
        o_ref[...] = jnp.dot(xn_ref[...], w_ref[...], preferred_element_type=F32,
                             precision=lax.Precision.HIGHEST)
    else:
        o_ref[...] = _dot(xn_ref[...], w_ref[...])


def _norm_matmul(x, g, w, *, tm, tn, precise=False):
    t, d = x.shape
    n = w.shape[1]
    return pl.pallas_call(
        functools.partial(_norm_matmul_kernel, precise=precise),
        grid=(t // tm, n // tn),
        in_specs=[pl.BlockSpec((tm, d), lambda i, j: (i, 0)),
                  pl.BlockSpec((1, d), lambda i, j: (0, 0)),
                  pl.BlockSpec((d, tn), lambda i, j: (0, j))],
        out_specs=pl.BlockSpec((tm, tn), lambda i, j: (i, j)),
        out_shape=jax.ShapeDtypeStruct((t, n), F32),
        scratch_shapes=[pltpu.VMEM((tm, d), F32 if precise else BF16)],
        compiler_params=_params("parallel", "arbitrary"),
        name="norm_matmul_f32" if precise else "norm_matmul",
    )(x, g.reshape(1, d), w)


def _matmul_res_kernel(a0_ref, a1_ref, w_ref, r_ref, o_ref, *, n0):
    i = pl.program_id(0)

    @pl.when(i < n0)
    def _():
        o_ref[...] = r_ref[...] + _dot(a0_ref[...], w_ref[...])

    @pl.when(i >= n0)
    def _():
        o_ref[...] = r_ref[...] + _dot(a1_ref[...], w_ref[...])


def _matmul_res(a0, a1, w, r, *, tm, tn):
    t0, k = a0.shape
    t = t0 + a1.shape[0]
    n = w.shape[1]
    n0 = t0 // tm
    return pl.pallas_call(
        functools.partial(_matmul_res_kernel, n0=n0),
        grid=(t // tm, n // tn),
        in_specs=[pl.BlockSpec((tm, k), lambda i, j: (jnp.minimum(i, n0 - 1), 0)),
                  pl.BlockSpec((tm, k), lambda i, j: (jnp.maximum(i - n0, 0), 0)),
                  pl.BlockSpec((k, tn), lambda i, j: (0, j)),
                  pl.BlockSpec((tm, tn), lambda i, j: (i, j))],
        out_specs=pl.BlockSpec((tm, tn), lambda i, j: (i, j)),
        out_shape=jax.ShapeDtypeStruct((t, n), F32),
        compiler_params=_params("parallel", "arbitrary"),
        name="matmul_res",
    )(a0, a1, w, r)


def _log_sigmoid(x):
    return jnp.minimum(x, 0.0) - jnp.log(1.0 + jnp.exp(-jnp.abs(x)))


def _mx(x):
    return x.astype(BF16) if x.shape[0] % 16 == 0 else x


def _mlstm_kernel(*refs, chunk, n_sub, has_state):
    if has_state:
        (q_ref, k_ref, v_ref, o_ref, gc_ref, gr_ref, bc_ref, br_ref, gh_ref,
         c0_ref, n0_ref, m0_ref, h_ref, c_ref, n_ref, m_ref) = refs
    else:
        (q_ref, k_ref, v_ref, o_ref, gc_ref, gr_ref, bc_ref, br_ref, gh_ref,
         h_ref, c_ref, n_ref, m_ref) = refs
    L = chunk

    @pl.when(pl.program_id(1) == 0)
    def _():
        if has_state:
            c_ref[...] = c0_ref[...]
            n_ref[...] = n0_ref[...]
            m_ref[...] = m0_ref[...]
        else:
            c_ref[...] = jnp.zeros_like(c_ref)
            n_ref[...] = jnp.zeros_like(n_ref)
            m_ref[...] = jnp.zeros_like(m_ref)

    t_idx = lax.broadcasted_iota(jnp.int32, (L, L), 0)
    s_idx = lax.broadcasted_iota(jnp.int32, (L, L), 1)
    causal = s_idx <= t_idx
    heads_out = [[None] * n_sub for _ in range(A_HEADS)]
    units = [(sub, h) for sub in range(n_sub) for h in range(A_HEADS)]
    st = [dict() for _ in units]

    for u, (sub, h) in zip(st, units):
        r0, r1 = sub * L, (sub + 1) * L
        if h == 0:
            gcol = gc_ref[sub] + bc_ref[...]
            gcol = GATE_SOFTCAP * jnp.tanh(gcol / GATE_SOFTCAP)
            grow = gr_ref[sub] + br_ref[...]
            grow = GATE_SOFTCAP * jnp.tanh(grow / GATE_SOFTCAP)
            fcol_all = _log_sigmoid(gcol)
            frow_all = _log_sigmoid(grow)
        u["q"] = q_ref[r0:r1, h * A_DK:(h + 1) * A_DK]
        u["k"] = k_ref[r0:r1, h * A_DK:(h + 1) * A_DK] * (A_DK ** -0.5)
        u["vb"] = _mx(v_ref[r0:r1, h * A_DV:(h + 1) * A_DV])
        u["i_col"] = gcol[:, h:h + 1]
        u["i_row"] = grow[h:h + 1, :]
        f_col = fcol_all[:, A_HEADS + h:A_HEADS + h + 1]
        f_row = frow_all[A_HEADS + h:A_HEADS + h + 1, :]
        u["c_prev"] = c_ref[sub, h]
        u["n_prev"] = n_ref[sub, h:h + 1, :]
        u["m_prev"] = m_ref[sub, h:h + 1, 0:1]
        u["b_col"] = jnp.sum(jnp.where(causal, f_row, 0.0), axis=1, keepdims=True)
        u["b_row"] = jnp.sum(jnp.where(t_idx <= s_idx, f_col, 0.0), axis=0, keepdims=True)
        qb = _mx(u["q"])
        u["qk"] = _dot_nt(qb, _mx(u["k"]))
        u["qc"] = _dot(qb, u["c_prev"].astype(qb.dtype))
        u["qn"] = jnp.sum(u["q"] * u["n_prev"], axis=1, keepdims=True)

    for u in st:
        b_end = u["b_col"][L - 1:L, :]
        u["log_intra"] = jnp.where(causal, u["b_col"] - u["b_row"] + u["i_row"], NEG_INF)
        u["log_inter"] = u["b_col"] + u["m_prev"]
        u["max_intra"] = jnp.max(u["log_intra"], axis=1, keepdims=True)
        u["log_w_col"] = b_end - u["b_col"] + u["i_col"]
        u["m_end"] = b_end + u["m_prev"]
        u["max_w"] = jnp.max(b_end - u["b_row"] + u["i_row"], axis=1, keepdims=True)

    for u in st:
        u["m_t"] = jnp.maximum(u["log_inter"], u["max_intra"])
        u["w_inter"] = jnp.exp(u["log_inter"] - u["m_t"])
        u["s"] = u["qk"] * jnp.exp(u["log_intra"] - u["m_t"])
        u["s_sum"] = jnp.sum(u["s"], axis=1, keepdims=True)
        u["m_new"] = jnp.maximum(u["m_end"], u["max_w"])
        u["decay"] = jnp.exp(u["m_end"] - u["m_new"])
        kw = u["k"] * jnp.exp(u["log_w_col"] - u["m_new"])
        u["kw_sum"] = jnp.sum(kw, axis=0, keepdims=True)
        u["sv"] = _dot(_mx(u["s"]), u["vb"])
        u["kv"] = _dot_tn(_mx(kw), u["vb"])

    for u, (sub, h) in zip(st, units):
        num = u["w_inter"] * u["qc"] + u["sv"]
        den = u["w_inter"] * u["qn"] + u["s_sum"]
        u["hh"] = num / jnp.maximum(jnp.abs(den), jnp.exp(-u["m_t"]))
        u["hh_ms"] = jnp.mean(u["hh"] * u["hh"], axis=-1, keepdims=True)
        c_ref[sub, h] = u["decay"] * u["c_prev"] + u["kv"]
        n_ref[sub, h:h + 1, :] = u["decay"] * u["n_prev"] + u["kw_sum"]
        m_ref[sub, h:h + 1, :] = jnp.broadcast_to(u["m_new"], (1, LANES))

    for u, (sub, h) in zip(st, units):
        r0, r1 = sub * L, (sub + 1) * L
        hn = u["hh"] * lax.rsqrt(u["hh_ms"] + EPS) * gh_ref[:, h * A_DV:(h + 1) * A_DV]
        heads_out[h][sub] = hn * jax.nn.sigmoid(o_ref[r0:r1, h * A_DV:(h + 1) * A_DV])

    for h in range(A_HEADS):
        rows = heads_out[h][0] if n_sub == 1 else jnp.concatenate(heads_out[h], axis=0)
        h_ref[:, h * A_DV:(h + 1) * A_DV] = rows.astype(h_ref.dtype)


def _mlstm(z, zg, b_gate, g_h, *, row0, n_seq, seq_len, chunk, n_sub, state=None):
    L = chunk
    nc = seq_len // L
    assert n_sub == 1 or nc == 1
    hd = A_HEADS * A_DK
    hv = A_HEADS * A_DV
    R = n_sub * L
    rb0 = row0 // R
    rows = n_seq * seq_len
    gates = zg[row0:row0 + rows, :2 * A_HEADS].reshape(n_seq * nc, L, 2 * A_HEADS)
    gates_t = jnp.swapaxes(gates, 1, 2)
    has_state = state is not None

    def rmap(col):
        return lambda s, c: (rb0 + s * nc + c, col)

    in_specs = [pl.BlockSpec((R, hd), rmap(0)),
                pl.BlockSpec((R, hd), rmap(1)),
                pl.BlockSpec((R, hv), rmap(hd * 2 // hv)),
                pl.BlockSpec((R, hv), rmap(hd * 2 // hv + 1)),
                pl.BlockSpec((n_sub, L, 2 * A_HEADS), lambda s, c: (s * nc + c, 0, 0)),
                pl.BlockSpec((n_sub, 2 * A_HEADS, L), lambda s, c: (s * nc + c, 0, 0)),
                pl.BlockSpec((1, 2 * A_HEADS), lambda s, c: (0, 0)),
                pl.BlockSpec((2 * A_HEADS, 1), lambda s, c: (0, 0)),
                pl.BlockSpec((1, hv), lambda s, c: (0, 0))]
    args = [z, z, z, z, gates, gates_t, b_gate.reshape(1, -1), b_gate.reshape(-1, 1), g_h.reshape(1, hv)]
    state_specs = [pl.BlockSpec((n_sub, A_HEADS, A_DK, A_DV), lambda s, c: (s, 0, 0, 0)),
                   pl.BlockSpec((n_sub, A_HEADS, A_DK), lambda s, c: (s, 0, 0)),
                   pl.BlockSpec((n_sub, A_HEADS, LANES), lambda s, c: (s, 0, 0))]
    if has_state:
        c0, n0, m0 = state
        in_specs += state_specs
        args += [c0, n0, jnp.broadcast_to(m0[..., None], m0.shape + (LANES,))]
    out_shape = (jax.ShapeDtypeStruct((rows, hv), BF16),
                 jax.ShapeDtypeStruct((n_seq, A_HEADS, A_DK, A_DV), F32),
                 jax.ShapeDtypeStruct((n_seq, A_HEADS, A_DK), F32),
                 jax.ShapeDtypeStruct((n_seq, A_HEADS, LANES), F32))
    h, c, n, m = pl.pallas_call(
        functools.partial(_mlstm_kernel, chunk=L, n_sub=n_sub, has_state=has_state),
        grid=(n_seq // n_sub, nc),
        in_specs=in_specs,
        out_specs=(pl.BlockSpec((R, hv), lambda s, c: (s * nc + c, 0)),) + tuple(state_specs),
        out_shape=out_shape,
        compiler_params=_params("parallel", "arbitrary"),
        name="mlstm_state" if has_state else "mlstm",
    )(*args)
    return h, c, n, m[..., 0]


def _alibi_slope(h):
    return 2.0 ** (-8.0 * (h + 1.0) / B_HEADS)


def _sink_softmax_all(logits, sinks):
    ms = [jnp.maximum(jnp.max(l, axis=-1, keepdims=True), s) for l, s in zip(logits, sinks)]
    ps = [jnp.exp(l - m) for l, m in zip(logits, ms)]
    dens = [jnp.sum(p, axis=-1, keepdims=True) + jnp.exp(s - m) for p, s, m in zip(ps, sinks, ms)]
    return [p / d for p, d in zip(ps, dens)]


def _swa_prompt_kernel(sink_ref, q_ref, kp_ref, ko_ref, vp_ref, vo_ref, gq_ref, gk_ref,
                       o_ref, kn_ref, *, nq):
    W = WINDOW
    t_idx = lax.broadcasted_iota(jnp.int32, (W, 2 * W), 0)
    s_idx = lax.broadcasted_iota(jnp.int32, (W, 2 * W), 1)
    dist = W + t_idx - s_idx
    in_band = (dist >= 0) & (dist <= W)
    in_band_first = in_band & ((s_idx >= W) | (pl.program_id(1) > 0))
    dist_f = dist.astype(F32)
    gq = gq_ref[...]
    gk = gk_ref[...]
    units = [(j, kh) for j in range(nq) for kh in range(B_KV_HEADS)]
    qs, ks, vs, kns = [], [], [], []
    k_raw = []
    for kh in range(B_KV_HEADS):
        lo, hi = kh * B_HD, (kh + 1) * B_HD
        k_raw.append(jnp.concatenate([kp_ref[:, lo:hi], ko_ref[:, lo:hi]], axis=0))
        vs.append(jnp.concatenate([vp_ref[:, lo:hi], vo_ref[:, lo:hi]], axis=0).astype(BF16))
    for kn in _rms_all(k_raw, gk):
        kns.append(kn[nq * W:])
        ks.append(kn.astype(BF16))
    q_raw = [q_ref[j * W:(j + 1) * W, h * B_HD:(h + 1) * B_HD]
             for j, kh in units for h in range(kh * B_GROUP, (kh + 1) * B_GROUP)]
    q_n = _rms_all(q_raw, gq)
    qs = [jnp.concatenate(q_n[u * B_GROUP:(u + 1) * B_GROUP], axis=0).astype(BF16)
          for u in range(len(units))]
    scores = [_dot_nt(q, ks[kh][j * W:(j + 2) * W]) for q, (j, kh) in zip(qs, units)]
    logits, sinks = [], []
    for s_all, (j, kh) in zip(scores, units):
        valid = in_band_first if j == 0 else in_band
        for g in range(B_GROUP):
            h = kh * B_GROUP + g
            s = s_all[g * W:(g + 1) * W] * (B_HD ** -0.5)
            logits.append(jnp.where(valid, s - _alibi_slope(h) * dist_f, NEG_INF))
            sinks.append(sink_ref[h])
    pieces = [p.astype(BF16) for p in _sink_softmax_all(logits, sinks)]
    probs = [jnp.concatenate(pieces[u * B_GROUP:(u + 1) * B_GROUP], axis=0) for u in range(len(units))]
    outs = [_dot(p, vs[kh][j * W:(j + 2) * W]) for p, (j, kh) in zip(probs, units)]
    for j in range(nq):
        o_ref[j * W:(j + 1) * W, :] = jnp.concatenate(
            [outs[j * B_KV_HEADS + kh][g * W:(g + 1) * W]
             for kh in range(B_KV_HEADS) for g in range(B_GROUP)], axis=-1).astype(o_ref.dtype)
    kn_ref[0] = jnp.concatenate(kns, axis=-1)


def _swa_prompt(z, g_q, g_k, sinks, *, n_seq, seq_len, nq):
    W = WINDOW
    nb = seq_len // (nq * W)
    bq = B_HEADS * B_HD
    bkv = B_KV_HEADS * B_HD
    kcol = bq // bkv

    def own(col):
        return lambda b, i: (b * nb + i, col)

    def prev(col):
        return lambda b, i: (jnp.maximum((b * nb + i) * nq - 1, 0), col)

    return pl.pallas_call(
        functools.partial(_swa_prompt_kernel, nq=nq),
        grid=(n_seq, nb),
        in_specs=[pl.BlockSpec(memory_space=pltpu.SMEM),
                  pl.BlockSpec((nq * W, bq), own(0)),
                  pl.BlockSpec((W, bkv), prev(kcol)),
                  pl.BlockSpec((nq * W, bkv), own(kcol)),
                  pl.BlockSpec((W, bkv), prev(kcol + 1)),
                  pl.BlockSpec((nq * W, bkv), own(kcol + 1)),
                  pl.BlockSpec((1, B_HD), lambda b, i: (0, 0)),
                  pl.BlockSpec((1, B_HD), lambda b, i: (0, 0))],
        out_specs=(pl.BlockSpec((nq * W, bq), own(0)),
                   pl.BlockSpec((1, W, bkv), lambda b, i: (b, 0, 0))),
        out_shape=(jax.ShapeDtypeStruct((n_seq * seq_len, bq), BF16),
                   jax.ShapeDtypeStruct((n_seq, W, bkv), F32)),
        compiler_params=_params("parallel", "arbitrary"),
        name="swa_prompt",
    )(sinks, z, z, z, z, z, g_q.reshape(1, B_HD), g_k.reshape(1, B_HD))


def _swa_sample_kernel(sink_ref, q_ref, k_ref, v_ref, ck_ref, cv_ref, gq_ref, gk_ref,
                       o_ref, cko_ref, cvo_ref, *, n_new, n_batch):
    W = WINDOW
    T = n_new
    R = B_HEADS * T
    GT = B_GROUP * T
    r_col = lax.broadcasted_iota(jnp.int32, (R, 1), 0)
    s_idx = lax.broadcasted_iota(jnp.int32, (R, W + T), 1)
    t_col = r_col
    slope_col = jnp.full((R, 1), _alibi_slope(0), F32)
    sink_col = jnp.full((R, 1), sink_ref[0], F32)
    for h in range(1, B_HEADS):
        in_later_head = r_col >= h * T
        t_col = jnp.where(in_later_head, r_col - h * T, t_col)
        slope_col = jnp.where(in_later_head, _alibi_slope(h), slope_col)
        sink_col = jnp.where(in_later_head, sink_ref[h], sink_col)
    dist = W + t_col - s_idx
    valid = (dist >= 0) & (dist <= W)
    bias = slope_col * dist.astype(F32)
    gq = gq_ref[...]
    gk = gk_ref[...]
    qn = _rms_all([q_ref[:, h * B_HD:(h + 1) * B_HD] for h in range(B_HEADS)], gq)
    kn = _rms_all([k_ref[:, kh * B_HD:(kh + 1) * B_HD] for kh in range(B_KV_HEADS)], gk)
    qs, ks, vs = [], [], []
    for bi in range(n_batch):
        r0, r1 = bi * T, (bi + 1) * T
        for kh in range(B_KV_HEADS):
            lo, hi = kh * B_HD, (kh + 1) * B_HD
            k_all = jnp.concatenate([ck_ref[bi, :, lo:hi], kn[kh][r0:r1]], axis=0)
            v_all = jnp.concatenate([cv_ref[bi, :, lo:hi], v_ref[r0:r1, lo:hi]], axis=0)
            cko_ref[bi, :, lo:hi] = k_all[T:]
            cvo_ref[bi, :, lo:hi] = v_all[T:]
            ks.append(k_all)
            vs.append(v_all)
            heads = range(kh * B_GROUP, (kh + 1) * B_GROUP)
            qs.append(jnp.concatenate([qn[h][r0:r1] for h in heads], axis=0))
    scores = [_dot_nt(q, k) for q, k in zip(qs, ks)]
    logits = []
    for bi in range(n_batch):
        s = jnp.concatenate(scores[bi * B_KV_HEADS:(bi + 1) * B_KV_HEADS], axis=0)
        logits.append(jnp.where(valid, s * (B_HD ** -0.5) - bias, NEG_INF))
    probs = _sink_softmax_all(logits, [sink_col] * n_batch)
    outs = [_dot(probs[i // B_KV_HEADS][(i % B_KV_HEADS) * GT:(i % B_KV_HEADS + 1) * GT], v)
            for i, v in enumerate(vs)]
    rows_out = []
    for bi in range(n_batch):
        rows_out.append(jnp.concatenate(
            [outs[bi * B_KV_HEADS + kh][g * T:(g + 1) * T]
             for kh in range(B_KV_HEADS) for g in range(B_GROUP)], axis=-1))
    o_ref[...] = jnp.concatenate(rows_out, axis=0).astype(o_ref.dtype)


def _swa_sample(z, cache_k, cache_v, g_q, g_k, sinks, *, row0, n_new, n_batch):
    W = WINDOW
    n_seq = cache_k.shape[0]
    bq = B_HEADS * B_HD
    bkv = B_KV_HEADS * B_HD
    kcol = bq // bkv
    rows = n_new * n_batch
    rb0 = row0 // rows
    ck = cache_k.reshape(n_seq, W, bkv)
    cv = cache_v.reshape(n_seq, W, bkv)
    cache_spec = pl.BlockSpec((n_batch, W, bkv), lambda i: (i, 0, 0))
    o, cko, cvo = pl.pallas_call(
        functools.partial(_swa_sample_kernel, n_new=n_new, n_batch=n_batch),
        grid=(n_seq // n_batch,),
        in_specs=[pl.BlockSpec(memory_space=pltpu.SMEM),
                  pl.BlockSpec((rows, bq), lambda i: (rb0 + i, 0)),
                  pl.BlockSpec((rows, bkv), lambda i: (rb0 + i, kcol)),
                  pl.BlockSpec((rows, bkv), lambda i: (rb0 + i, kcol + 1)),
                  cache_spec, cache_spec,
                  pl.BlockSpec((1, B_HD), lambda i: (0, 0)),
                  pl.BlockSpec((1, B_HD), lambda i: (0, 0))],
        out_specs=(pl.BlockSpec((rows, bq), lambda i: (i, 0)), cache_spec, cache_spec),
        out_shape=(jax.ShapeDtypeStruct((n_seq * n_new, bq), BF16),
                   jax.ShapeDtypeStruct(ck.shape, cache_k.dtype),
                   jax.ShapeDtypeStruct(cv.shape, cache_v.dtype)),
        compiler_params=_params("parallel"),
        name="swa_sample",
    )(sinks, z, z, z, ck, cv, g_q.reshape(1, B_HD), g_k.reshape(1, B_HD))
    return o, cko.reshape(cache_k.shape), cvo.reshape(cache_v.shape)


def _top_ranked(s, k):
    rows = s.shape[0]
    ridx = lax.broadcasted_iota(jnp.int32, s.shape, 0).astype(F32)
    rank = jnp.full(s.shape, float(k), F32)
    vals = []
    for r in range(k):
        m = jnp.max(s, axis=0, keepdims=True)
        vals.append(m)
        first = ridx == jnp.min(jnp.where(s == m, ridx, float(rows)), axis=0, keepdims=True)
        rank = jnp.where(first, float(r), rank)
        s = jnp.where(first, NEG_INF, s)
    return vals, rank


def _top_ranked_distinct(s, k):
    rank = jnp.full(s.shape, float(k), F32)
    vals = []
    for r in range(k):
        m = jnp.max(s, axis=0, keepdims=True)
        vals.append(m)
        hit = s == m
        rank = jnp.where(hit, float(r), rank)
        s = jnp.where(hit, NEG_INF, s)
    return vals, rank


def _ranked_ok(rank, k):
    n_ranked = jnp.sum(jnp.where(rank < k, 1.0, 0.0), axis=0, keepdims=True)
    return jnp.where(n_ranked == k, 1.0, 0.0)


_CAND_PAIRS = [(a, b) for a in range(P_TOPK) for b in range(P_TOPK) if (a + 1) * (b + 1) <= P_TOPK]
_CAND_ROWS = -(-len(_CAND_PAIRS) // 8) * 8


TAB_COUNT0, TAB_GATE0 = 0, 1
TAB_RANK1, TAB_GATE1 = 0, 1


def _peer_route_kernel(x_ref, g_ref, wqt_ref, keys_ref, xnt_ref, tab0_ref, tab1_ref,
                       qt_ref, cand_ref, tied_ref):
    xn = _rms(x_ref[...], g_ref[...])
    xnt = xn.T.astype(BF16)
    xnt_ref[...] = xnt
    qt_ref[...] = _dot(wqt_ref[...], xnt)
    half = keys_ref.shape[2]
    cand_ref[...] = jnp.full(cand_ref.shape, NEG_INF, F32)

    def route_head(h, top_fn):
        row0 = 2 * h * half if isinstance(h, int) else pl.multiple_of(2 * h * half, half)
        q0 = qt_ref[pl.ds(row0, half), :]
        q1 = qt_ref[pl.ds(row0 + half, half), :]
        s0 = _dot(keys_ref[2 * h], q0)
        s1 = _dot(keys_ref[2 * h + 1], q1)
        top0, rank0 = top_fn(s0, P_TOPK)
        top1, rank1 = top_fn(s1, P_TOPK)
        for r, (a, b) in enumerate(_CAND_PAIRS):
            cand_ref[r:r + 1, :] = top0[a] + top1[b]
        best, cand_rank = top_fn(cand_ref[...], P_TOPK)
        z = jnp.zeros_like(best[0])
        for c in best:
            z = z + jnp.exp(c - best[0])
        taken = jnp.where(cand_rank < P_TOPK, 1.0, 0.0)
        count0 = jnp.zeros_like(s0)
        for a in range(P_TOPK):
            rows = [r for r, (ca, _) in enumerate(_CAND_PAIRS) if ca == a]
            n_sel = jnp.sum(taken[rows[0]:rows[-1] + 1], axis=0, keepdims=True)
            count0 = jnp.where(rank0 == a, n_sel, count0)
        tab0_ref[2 * h + TAB_COUNT0] = 2.0 * count0 - 1.0
        tab0_ref[2 * h + TAB_GATE0] = jnp.exp(s0 - top0[0]) / z
        tab1_ref[2 * h + TAB_RANK1] = (2.0 * rank1).astype(BF16)
        tab1_ref[2 * h + TAB_GATE1] = jnp.exp(s1 - top1[0]).astype(BF16)
        return _ranked_ok(rank0, P_TOPK) * _ranked_ok(rank1, P_TOPK) * _ranked_ok(cand_rank, P_TOPK)

    for h in range(P_HEADS):
        ok = route_head(h, _top_ranked_distinct)
        tied_ref[h] = (jnp.min(ok) < 1.0).astype(jnp.int32)

    def redo(h, carry):
        @pl.when(tied_ref[h] != 0)
        def _():
            route_head(h, _top_ranked)
        return carry

    lax.fori_loop(0, P_HEADS, redo, 0)


def _peer_route(x, g, wqt, keys, *, tt):
    t, d = x.shape
    nq = wqt.shape[0]
    hp, nk, half = keys.shape
    tab_spec = pl.BlockSpec((2 * P_HEADS, nk, tt), lambda i: (0, 0, i))
    return pl.pallas_call(
        _peer_route_kernel,
        grid=(t // tt,),
        in_specs=[pl.BlockSpec((tt, d), lambda i: (i, 0)),
                  pl.BlockSpec((1, d), lambda i: (0, 0)),
                  pl.BlockSpec((nq, d), lambda i: (0, 0)),
                  pl.BlockSpec((hp, nk, half), lambda i: (0, 0, 0))],
        out_specs=(pl.BlockSpec((d, tt), lambda i: (0, i)), tab_spec, tab_spec),
        out_shape=(jax.ShapeDtypeStruct((d, t), BF16),
                   jax.ShapeDtypeStruct((2 * P_HEADS, nk, t), F32),
                   jax.ShapeDtypeStruct((2 * P_HEADS, nk, t), BF16)),
        scratch_shapes=[pltpu.VMEM((nq, tt), F32), pltpu.VMEM((_CAND_ROWS, tt), F32),
                        pltpu.SMEM((P_HEADS,), jnp.int32)],
        compiler_params=_params("parallel"),
        name="peer_route",
    )(x, g.reshape(1, d), wqt, keys)


def _gelu(x):
    return 0.5 * x * (1.0 + lax.erf(x * math.sqrt(0.5)))


def _rows_bf16(row, n):
    packed = jnp.broadcast_to(row, (BF16_SUBLANES, LANES)).astype(BF16)
    return jnp.concatenate([packed] * (n // BF16_SUBLANES), axis=0)


def _peer_expert_kernel(xnt_ref, tab0_ref, tab1_ref, u_ref, vt_ref, yt_ref, w0_ref, *, ne, nc):
    et = pl.program_id(1)
    n_et = pl.num_programs(1)
    tt = xnt_ref.shape[1]
    n_i = ne // N_KEYS
    assert n_i == SUBLANES and nc % LANES == 0
    n_chunk = tt // nc
    row_group = 4
    blocks_rc = [(r, c) for r in range(n_i) for c in range(nc // LANES)]

    def hidden(k):
        return _dot(u_ref[...], xnt_ref[:, k * nc:(k + 1) * nc])

    def gates(k, tile):
        rows8 = pl.ds(pl.multiple_of(tile * n_i, SUBLANES), SUBLANES)
        blocks = {}
        for c in range(nc // LANES):
            cs = slice(k * nc + c * LANES, k * nc + (c + 1) * LANES)
            for r0 in range(0, n_i, row_group):
                rows = range(r0, r0 + row_group)
                w = {r: jnp.zeros((N_KEYS, LANES), BF16) for r in rows}
                for h in range(P_HEADS):
                    rank1 = tab1_ref[2 * h + TAB_RANK1, :, cs]
                    gate1 = tab1_ref[2 * h + TAB_GATE1, :, cs]
                    count0 = tab0_ref[2 * h + TAB_COUNT0, rows8, cs]
                    gate0 = tab0_ref[2 * h + TAB_GATE0, rows8, cs]
                    for r in rows:
                        sel_gate1 = jnp.minimum(
                            jnp.maximum(_rows_bf16(count0[r:r + 1], N_KEYS) - rank1, 0.0), gate1)
                        w[r] = w[r] + sel_gate1 * _rows_bf16(gate0[r:r + 1], N_KEYS)
                for r in rows:
                    blocks[r, c] = w[r]
        return blocks

    def activate(hid, w):
        return jnp.concatenate(
            [jnp.concatenate(
                [_gelu(hid[r * N_KEYS:(r + 1) * N_KEYS, c * LANES:(c + 1) * LANES]).astype(BF16) * w[r, c]
                 for c in range(nc // LANES)], axis=1)
             for r in range(n_i)], axis=0)

    def accumulate(k, a):
        yt_ref[:, k * nc:(k + 1) * nc] += _dot(vt_ref[...], a)

    def store_gates(w):
        for r, c in blocks_rc:
            w0_ref[r * N_KEYS:(r + 1) * N_KEYS, c * LANES:(c + 1) * LANES] = w[r, c]

    @pl.when(et == 0)
    def _():
        yt_ref[...] = jnp.zeros_like(yt_ref)
        store_gates(gates(0, et))

    w = {(r, c): w0_ref[r * N_KEYS:(r + 1) * N_KEYS, c * LANES:(c + 1) * LANES] for r, c in blocks_rc}
    hid = hidden(0)
    a_prev = None
    for k in range(n_chunk):
        last = k + 1 == n_chunk
        hid_next = None if last else hidden(k + 1)
        if a_prev is not None:
            accumulate(k - 1, a_prev)
        w_next = None if last else gates(k + 1, et)
        a_prev = activate(hid, w)
        hid, w = hid_next, w_next
    w_ahead = gates(0, jnp.minimum(et + 1, n_et - 1))
    accumulate(n_chunk - 1, a_prev)
    store_gates(w_ahead)


def _peer_experts(xnt, tab0, tab1, u, vt, *, tt, ne, nc):
    d, t = xnt.shape
    n_tab, nk, _ = tab0.shape
    n_exp = u.shape[0]
    once = pl.Buffered(1)
    tab_spec = pl.BlockSpec((n_tab, nk, tt), lambda i, e: (0, 0, i), pipeline_mode=once)
    return pl.pallas_call(
        functools.partial(_peer_expert_kernel, ne=ne, nc=nc),
        grid=(t // tt, n_exp // ne),
        in_specs=[pl.BlockSpec((d, tt), lambda i, e: (0, i), pipeline_mode=once),
                  tab_spec, tab_spec,
                  pl.BlockSpec((ne, d), lambda i, e: (e, 0)),
                  pl.BlockSpec((d, ne), lambda i, e: (0, e))],
        out_specs=pl.BlockSpec((d, tt), lambda i, e: (0, i), pipeline_mode=once),
        out_shape=jax.ShapeDtypeStruct((d, t), F32),
        scratch_shapes=[pltpu.VMEM((ne, nc), BF16)],
        compiler_params=_params("parallel", "arbitrary"),
        name="peer_experts",
    )(xnt, tab0, tab1, u, vt)


def _add_transposed_kernel(x_ref, yt_ref, o_ref):
    o_ref[...] = x_ref[...] + yt_ref[...].T


def _add_transposed(x, yt, *, tt, row0=0, rows=None):
    t, d = x.shape
    rows = t if rows is None else rows
    b0 = row0 // tt
    return pl.pallas_call(
        _add_transposed_kernel,
        grid=(rows // tt,),
        in_specs=[pl.BlockSpec((tt, d), lambda i: (b0 + i, 0)),
                  pl.BlockSpec((d, tt), lambda i: (0, b0 + i))],
        out_specs=pl.BlockSpec((tt, d), lambda i: (i, 0)),
        out_shape=jax.ShapeDtypeStruct((rows, d), F32),
        compiler_params=_params("parallel"),
        name="add_transposed",
    )(x, yt)


def _cast_kernel(x_ref, o_ref, *, transpose):
    x = x_ref[...]
    o_ref[...] = (x.T if transpose else x).astype(o_ref.dtype)


def _layer_cast(w, layer, dtype, *, rows, transpose):
    _, n, d = w.shape
    return pl.pallas_call(
        functools.partial(_cast_kernel, transpose=transpose),
        grid=(n // rows,),
        in_specs=[pl.BlockSpec((None, rows, d), lambda i: (layer, i, 0))],
        out_specs=(pl.BlockSpec((d, rows), lambda i: (0, i)) if transpose
                   else pl.BlockSpec((rows, d), lambda i: (i, 0))),
        out_shape=jax.ShapeDtypeStruct((d, n) if transpose else (n, d), dtype),
        compiler_params=_params("parallel"),
        name="layer_cast_t" if transpose else "layer_cast",
    )(w)


def _peer(x, g, peer_wq, keys, peer_u, peer_v, layer, *, tt_route, tt, ne):
    hp = keys.shape[0] * keys.shape[1]
    wqt = _layer_cast(peer_wq, layer, BF16, rows=512, transpose=True)
    xnt, tab0, tab1 = _peer_route(x, g, wqt, keys.reshape(hp, keys.shape[2], keys.shape[3]), tt=tt_route)
    u = _layer_cast(peer_u, layer, BF16, rows=512, transpose=False)
    vt = _layer_cast(peer_v, layer, BF16, rows=512, transpose=True)
    return _peer_experts(xnt, tab0, tab1, u, vt, tt=tt, ne=ne, nc=min(tt, 4 * LANES))


def _pick_tile(n, pref):
    t = pref
    while n % t:
        t //= 2
    return t


def kernel(x_prompt, x_sample, state_c, state_n, state_m, cache_k, cache_v, norm_mix, w_in_a, b_gate_a,
           norm_h_a, w_out_a, w_in_b, g_q_b, g_k_b, sink_b, w_out_b, norm_ffn, peer_wq, peer_keys,
           peer_u, peer_v):
    bp, sp, d = x_prompt.shape
    bs, ss, _ = x_sample.shape
    tp, ts = bp * sp, bs * ss
    t = tp + ts
    x = jnp.concatenate([x_prompt.reshape(tp, d), x_sample.reshape(ts, d)], axis=0)
    tm = _pick_tile(math.gcd(tp, ts), 1024)
    tr = min(tm, 256)
    te = _pick_tile(t, 1024)
    n_qkvo = 2 * A_HEADS * (A_DK + A_DV)
    chunk_p = _pick_tile(sp, 256)

    w_in = w_in_a[0]
    z = _norm_matmul(x, norm_mix[0], w_in[:, :n_qkvo].astype(BF16), tm=tm, tn=1024)
    w_gate = jnp.pad(w_in[:, n_qkvo:], ((0, 0), (0, LANES - 2 * A_HEADS)))
    zg = _norm_matmul(x, norm_mix[0], w_gate, tm=tm, tn=LANES, precise=True)
    h_p, c_p, n_p, m_p = _mlstm(z, zg, b_gate_a[0], norm_h_a[0], row0=0, n_seq=bp, seq_len=sp,
                                chunk=chunk_p, n_sub=1)
    h_s, c_s, n_s, m_s = _mlstm(z, zg, b_gate_a[0], norm_h_a[0], row0=tp, n_seq=bs, seq_len=ss,
                                chunk=ss, n_sub=4, state=(state_c[0], state_n[0], state_m[0]))
    x = _matmul_res(h_p, h_s, w_out_a[0].astype(BF16), x, tm=tm, tn=1024)
    yt = _peer(x, norm_ffn[0], peer_wq, peer_keys[0], peer_u, peer_v, 0, tt_route=tr, tt=te, ne=1024)
    x = _add_transposed(x, yt, tt=tr)

    z = _norm_matmul(x, norm_mix[1], w_in_b[0].astype(BF16), tm=tm, tn=1024)
    o_p, kn_p = _swa_prompt(z, g_q_b[0], g_k_b[0], sink_b[0], n_seq=bp, seq_len=sp,
                            nq=2 if sp % (2 * WINDOW) == 0 else 1)
    o_s, ck_s, cv_s = _swa_sample(z, cache_k[0], cache_v[0], g_q_b[0], g_k_b[0], sink_b[0],
                                  row0=tp, n_new=ss, n_batch=8)
    x = _matmul_res(o_p, o_s, w_out_b[0].astype(BF16), x, tm=tm, tn=1024)
    yt = _peer(x, norm_ffn[1], peer_wq, peer_keys[1], peer_u, peer_v, 1, tt_route=tr, tt=te, ne=1024)
    y_p = _add_transposed(x, yt, tt=tr, row0=0, rows=tp)
    y_s = _add_transposed(x, yt, tt=tr, row0=tp, rows=ts)

    bq = B_HEADS * B_HD
    bkv = B_KV_HEADS * B_HD
    v_p = z[:tp, bq + bkv:].reshape(bp, sp, bkv)[:, sp - WINDOW:]
    cache_shape = (1, bp, WINDOW, B_KV_HEADS, B_HD)
    return (y_p.reshape(bp, sp, d), y_s.reshape(bs, ss, d),
            c_p[None], n_p[None], m_p[None],
            kn_p.reshape(cache_shape), v_p.reshape(cache_shape),
            c_s[None], n_s[None], m_s[None], ck_s[None], cv_s[None])
```

```python
import functools
import math

import jax
import jax.numpy as jnp
from jax import lax
from jax.experimental import pallas as pl
from jax.experimental.pallas import tpu as pltpu

F32 = jnp.float32
BF16 = jnp.bfloat16
EPS = 1e-6
NEG_INF = float("-inf")

A_HEADS = 8
A_DK = 128
A_DV = 256
GATE_SOFTCAP = 15.0
B_HEADS = 32
B_KV_HEADS = 8
B_HD = 64
B_GROUP = B_HEADS // B_KV_HEADS
WINDOW = 128
P_HEADS = 8
N_KEYS = 128
P_TOPK = 16

LANES = 128
SUBLANES = 8
BF16_SUBLANES = 16
VMEM_LIMIT = 58 * 1024 * 1024


def _params(*sem):
    return pltpu.CompilerParams(dimension_semantics=sem, vmem_limit_bytes=VMEM_LIMIT)


def _dot(a, b):
    return jnp.dot(a, b, preferred_element_type=F32)


def _dot_nt(a, b):
    return lax.dot_general(a, b, (((1,), (1,)), ((), ())), preferred_element_type=F32)


def _dot_tn(a, b):
    return lax.dot_general(a, b, (((0,), (0,)), ((), ())), preferred_element_type=F32)


def _rms(x, g):
    return x * lax.rsqrt(jnp.mean(x * x, axis=-1, keepdims=True) + EPS) * g


def _rms_all(xs, g):
    ms = [jnp.mean(x * x, axis=-1, keepdims=True) for x in xs]
    return [x * lax.rsqrt(m + EPS) * g for x, m in zip(xs, ms)]


def _norm_matmul_kernel(x_ref, g_ref, w_ref, o_ref, xn_ref, *, precise):
    @pl.when(pl.program_id(1) == 0)
    def _():
        xn_ref[...] = _rms(x_ref[...], g_ref[...]).astype(xn_ref.dtype)

    if precise:
        o_ref[...] = jnp.dot(xn_ref[...], w_ref[...], preferred_element_type=F32,
                             precision=lax.Precision.HIGHEST)
    else:
        o_ref[...] = _dot(xn_ref[...], w_ref[...])


def _norm_matmul_gates_kernel(x_ref, g_ref, w_ref, wgh_ref, wgl_ref, o_ref, og_ref, xn_ref):
    @pl.when(pl.program_id(1) == 0)
    def _():
        xn = _rms(x_ref[...], g_ref[...])
        x_hi = xn.astype(BF16)
        xn_ref[...] = x_hi
        x_lo = (xn - x_hi.astype(F32)).astype(BF16)
        og_ref[...] = _dot(x_hi, wgh_ref[...]) + (_dot(x_lo, wgh_ref[...]) + _dot(x_hi, wgl_ref[...]))

    o_ref[...] = _dot(xn_ref[...], w_ref[...])


def _norm_matmul_gates(x, g, w, w_gate, *, tm, tn):
    t, d = x.shape
    n = w.shape[1]
    ng = w_gate.shape[1]
    wg_hi = w_gate.astype(BF16)
    wg_lo = (w_gate - wg_hi.astype(F32)).astype(BF16)
    gate_spec = pl.BlockSpec((d, ng), lambda i, j: (0, 0))
    return pl.pallas_call(
        _norm_matmul_gates_kernel,
        grid=(t // tm, n // tn),
        in_specs=[pl.BlockSpec((tm, d), lambda i, j: (i, 0)),
                  pl.BlockSpec((1, d), lambda i, j: (0, 0)),
                  pl.BlockSpec((d, tn), lambda i, j: (0, j)),
                  gate_spec, gate_spec],
        out_specs=(pl.BlockSpec((tm, tn), lambda i, j: (i, j)),
                   pl.BlockSpec((tm, ng), lambda i, j: (i, 0))),
        out_shape=(jax.ShapeDtypeStruct((t, n), F32), jax.ShapeDtypeStruct((t, ng), F32)),
        scratch_shapes=[pltpu.VMEM((tm, d), BF16)],
        compiler_params=_params("parallel", "arbitrary"),
        name="norm_matmul_gates",
    )(x, g.reshape(1, d), w, wg_hi, wg_lo)


def _norm_matmul(x, g, w, *, tm, tn, precise=False):
    t, d = x.shape
    n = w.shape[1]
    return pl.pallas_call(
        functools.partial(_norm_matmul_kernel, precise=precise),
        grid=(t // tm, n // tn),
        in_specs=[pl.BlockSpec((tm, d), lambda i, j: (i, 0)),
                  pl.BlockSpec((1, d), lambda i, j: (0, 0)),
                  pl.BlockSpec((d, tn), lambda i, j: (0, j))],
        out_specs=pl.BlockSpec((tm, tn), lambda i, j: (i, j)),
        out_shape=jax.ShapeDtypeStruct((t, n), F32),
        scratch_shapes=[pltpu.VMEM((tm, d), F32 if precise else BF16)],
        compiler_params=_params("parallel", "arbitrary"),
        name="norm_matmul_f32" if precise else "norm_matmul",
    )(x, g.reshape(1, d), w)


def _matmul_res_kernel(a0_ref, a1_ref, w_ref, r_ref, o_ref, *, n0):
    i = pl.program_id(0)

    @pl.when(i < n0)
    def _():
        o_ref[...] = r_ref[...] + _dot(a0_ref[...], w_ref[...])

    @pl.when(i >= n0)
    def _():
        o_ref[...] = r_ref[...] + _dot(a1_ref[...], w_ref[...])


def _matmul_res(a0, a1, w, r, *, tm, tn):
    t0, k = a0.shape
    t = t0 + a1.shape[0]
    n = w.shape[1]
    n0 = t0 // tm
    return pl.pallas_call(
        functools.partial(_matmul_res_kernel, n0=n0),
        grid=(t // tm, n // tn),
        in_specs=[pl.BlockSpec((tm, k), lambda i, j: (jnp.minimum(i, n0 - 1), 0)),
                  pl.BlockSpec((tm, k), lambda i, j: (jnp.maximum(i - n0, 0), 0)),
                  pl.BlockSpec((k, tn), lambda i, j: (0, j)),
                  pl.BlockSpec((tm, tn), lambda i, j: (i, j))],
        out_specs=pl.BlockSpec((tm, tn), lambda i, j: (i, j)),
        out_shape=jax.ShapeDtypeStruct((t, n), F32),
        compiler_params=_params("parallel", "arbitrary"),
        name="matmul_res",
    )(a0, a1, w, r)


def _log_sigmoid(x):
    return jnp.minimum(x, 0.0) - jnp.log(1.0 + jnp.exp(-jnp.abs(x)))


def _mx(x):
    return x.astype(BF16) if x.shape[0] % 16 == 0 else x


def _mlstm_kernel(*refs, chunk, n_sub, has_state):
    if has_state:
        (q_ref, k_ref, v_ref, o_ref, gc_ref, gr_ref, bc_ref, br_ref, gh_ref,
         c0_ref, n0_ref, m0_ref, h_ref, c_ref, n_ref, m_ref) = refs
    else:
        (q_ref, k_ref, v_ref, o_ref, gc_ref, gr_ref, bc_ref, br_ref, gh_ref,
         h_ref, c_ref, n_ref, m_ref) = refs
    L = chunk

    @pl.when(pl.program_id(1) == 0)
    def _():
        if has_state:
            c_ref[...] = c0_ref[...]
            n_ref[...] = n0_ref[...]
            m_ref[...] = m0_ref[...]
        else:
            c_ref[...] = jnp.zeros_like(c_ref)
            n_ref[...] = jnp.zeros_like(n_ref)
            m_ref[...] = jnp.zeros_like(m_ref)

    t_idx = lax.broadcasted_iota(jnp.int32, (L, L), 0)
    s_idx = lax.broadcasted_iota(jnp.int32, (L, L), 1)
    causal = s_idx <= t_idx
    heads_out = [[None] * n_sub for _ in range(A_HEADS)]
    units = [(sub, h) for sub in range(n_sub) for h in range(A_HEADS)]
    st = [dict() for _ in units]

    for u, (sub, h) in zip(st, units):
        r0, r1 = sub * L, (sub + 1) * L
        if h == 0:
            gcol = gc_ref[sub] + bc_ref[...]
            gcol = GATE_SOFTCAP * jnp.tanh(gcol / GATE_SOFTCAP)
            grow = gr_ref[sub] + br_ref[...]
            grow = GATE_SOFTCAP * jnp.tanh(grow / GATE_SOFTCAP)
            fcol_all = _log_sigmoid(gcol)
            frow_all = _log_sigmoid(grow)
        u["q"] = q_ref[r0:r1, h * A_DK:(h + 1) * A_DK]
        u["k"] = k_ref[r0:r1, h * A_DK:(h + 1) * A_DK] * (A_DK ** -0.5)
        u["vb"] = _mx(v_ref[r0:r1, h * A_DV:(h + 1) * A_DV])
        u["i_col"] = gcol[:, h:h + 1]
        u["i_row"] = grow[h:h + 1, :]
        f_col = fcol_all[:, A_HEADS + h:A_HEADS + h + 1]
        f_row = frow_all[A_HEADS + h:A_HEADS + h + 1, :]
        u["c_prev"] = c_ref[sub, h]
        u["n_prev"] = n_ref[sub, h:h + 1, :]
        u["m_prev"] = m_ref[sub, h:h + 1, 0:1]
        u["b_col"] = jnp.sum(jnp.where(causal, f_row, 0.0), axis=1, keepdims=True)
        u["b_row"] = jnp.sum(jnp.where(t_idx <= s_idx, f_col, 0.0), axis=0, keepdims=True)
        qb = _mx(u["q"])
        u["qk"] = _dot_nt(qb, _mx(u["k"]))
        u["qc"] = _dot(qb, u["c_prev"].astype(qb.dtype))
        u["qn"] = jnp.sum(u["q"] * u["n_prev"], axis=1, keepdims=True)

    for u in st:
        b_end = u["b_col"][L - 1:L, :]
        u["log_intra"] = jnp.where(causal, u["b_col"] - u["b_row"] + u["i_row"], NEG_INF)
        u["log_inter"] = u["b_col"] + u["m_prev"]
        u["max_intra"] = jnp.max(u["log_intra"], axis=1, keepdims=True)
        u["log_w_col"] = b_end - u["b_col"] + u["i_col"]
        u["m_end"] = b_end + u["m_prev"]
        u["max_w"] = jnp.max(b_end - u["b_row"] + u["i_row"], axis=1, keepdims=True)

    for u in st:
        u["m_t"] = jnp.maximum(u["log_inter"], u["max_intra"])
        u["w_inter"] = jnp.exp(u["log_inter"] - u["m_t"])
        u["s"] = u["qk"] * jnp.exp(u["log_intra"] - u["m_t"])
        u["s_sum"] = jnp.sum(u["s"], axis=1, keepdims=True)
        u["m_new"] = jnp.maximum(u["m_end"], u["max_w"])
        u["decay"] = jnp.exp(u["m_end"] - u["m_new"])
        kw = u["k"] * jnp.exp(u["log_w_col"] - u["m_new"])
        u["kw_sum"] = jnp.sum(kw, axis=0, keepdims=True)
        u["sv"] = _dot(_mx(u["s"]), u["vb"])
        u["kv"] = _dot_tn(_mx(kw), u["vb"])

    for u, (sub, h) in zip(st, units):
        num = u["w_inter"] * u["qc"] + u["sv"]
        den = u["w_inter"] * u["qn"] + u["s_sum"]
        u["hh"] = num / jnp.maximum(jnp.abs(den), jnp.exp(-u["m_t"]))
        u["hh_ms"] = jnp.mean(u["hh"] * u["hh"], axis=-1, keepdims=True)
        c_ref[sub, h] = u["decay"] * u["c_prev"] + u["kv"]
        n_ref[sub, h:h + 1, :] = u["decay"] * u["n_prev"] + u["kw_sum"]
        m_ref[sub, h:h + 1, :] = jnp.broadcast_to(u["m_new"], (1, LANES))

    for u, (sub, h) in zip(st, units):
        r0, r1 = sub * L, (sub + 1) * L
        hn = u["hh"] * lax.rsqrt(u["hh_ms"] + EPS) * gh_ref[:, h * A_DV:(h + 1) * A_DV]
        heads_out[h][sub] = hn * jax.nn.sigmoid(o_ref[r0:r1, h * A_DV:(h + 1) * A_DV])

    for h in range(A_HEADS):
        rows = heads_out[h][0] if n_sub == 1 else jnp.concatenate(heads_out[h], axis=0)
        h_ref[:, h * A_DV:(h + 1) * A_DV] = rows.astype(h_ref.dtype)


def _mlstm(z, zg, b_gate, g_h, *, row0, n_seq, seq_len, chunk, n_sub, state=None):
    L = chunk
    nc = seq_len // L
    assert n_sub == 1 or nc == 1
    hd = A_HEADS * A_DK
    hv = A_HEADS * A_DV
    R = n_sub * L
    rb0 = row0 // R
    rows = n_seq * seq_len
    gates = zg[row0:row0 + rows, :2 * A_HEADS].reshape(n_seq * nc, L, 2 * A_HEADS)
    gates_t = jnp.swapaxes(gates, 1, 2)
    has_state = state is not None

    def rmap(col):
        return lambda s, c: (rb0 + s * nc + c, col)

    in_specs = [pl.BlockSpec((R, hd), rmap(0)),
                pl.BlockSpec((R, hd), rmap(1)),
                pl.BlockSpec((R, hv), rmap(hd * 2 // hv)),
                pl.BlockSpec((R, hv), rmap(hd * 2 // hv + 1)),
                pl.BlockSpec((n_sub, L, 2 * A_HEADS), lambda s, c: (s * nc + c, 0, 0)),
                pl.BlockSpec((n_sub, 2 * A_HEADS, L), lambda s, c: (s * nc + c, 0, 0)),
                pl.BlockSpec((1, 2 * A_HEADS), lambda s, c: (0, 0)),
                pl.BlockSpec((2 * A_HEADS, 1), lambda s, c: (0, 0)),
                pl.BlockSpec((1, hv), lambda s, c: (0, 0))]
    args = [z, z, z, z, gates, gates_t, b_gate.reshape(1, -1), b_gate.reshape(-1, 1), g_h.reshape(1, hv)]
    state_specs = [pl.BlockSpec((n_sub, A_HEADS, A_DK, A_DV), lambda s, c: (s, 0, 0, 0)),
                   pl.BlockSpec((n_sub, A_HEADS, A_DK), lambda s, c: (s, 0, 0)),
                   pl.BlockSpec((n_sub, A_HEADS, LANES), lambda s, c: (s, 0, 0))]
    if has_state:
        c0, n0, m0 = state
        in_specs += state_specs
        args += [c0, n0, jnp.broadcast_to(m0[..., None], m0.shape + (LANES,))]
    out_shape = (jax.ShapeDtypeStruct((rows, hv), BF16),
                 jax.ShapeDtypeStruct((n_seq, A_HEADS, A_DK, A_DV), F32),
                 jax.ShapeDtypeStruct((n_seq, A_HEADS, A_DK), F32),
                 jax.ShapeDtypeStruct((n_seq, A_HEADS, LANES), F32))
    h, c, n, m = pl.pallas_call(
        functools.partial(_mlstm_kernel, chunk=L, n_sub=n_sub, has_state=has_state),
        grid=(n_seq // n_sub, nc),
        in_specs=in_specs,
        out_specs=(pl.BlockSpec((R, hv), lambda s, c: (s * nc + c, 0)),) + tuple(state_specs),
        out_shape=out_shape,
        compiler_params=_params("parallel", "arbitrary"),
        name="mlstm_state" if has_state else "mlstm",
    )(*args)
    return h, c, n, m[..., 0]


def _alibi_slope(h):
    return 2.0 ** (-8.0 * (h + 1.0) / B_HEADS)


def _sink_softmax_all(logits, sinks):
    ms = [jnp.maximum(jnp.max(l, axis=-1, keepdims=True), s) for l, s in zip(logits, sinks)]
    ps = [jnp.exp(l - m) for l, m in zip(logits, ms)]
    dens = [jnp.sum(p, axis=-1, keepdims=True) + jnp.exp(s - m) for p, s, m in zip(ps, sinks, ms)]
    return [p / d for p, d in zip(ps, dens)]


def _swa_prompt_kernel(sink_ref, q_ref, kp_ref, ko_ref, vp_ref, vo_ref, gq_ref, gk_ref,
                       o_ref, kn_ref, *, nq):
    W = WINDOW
    t_idx = lax.broadcasted_iota(jnp.int32, (W, 2 * W), 0)
    s_idx = lax.broadcasted_iota(jnp.int32, (W, 2 * W), 1)
    dist = W + t_idx - s_idx
    in_band = (dist >= 0) & (dist <= W)
    in_band_first = in_band & ((s_idx >= W) | (pl.program_id(1) > 0))
    dist_f = dist.astype(F32)
    gq = gq_ref[...]
    gk = gk_ref[...]
    units = [(j, kh) for j in range(nq) for kh in range(B_KV_HEADS)]
    qs, ks, vs, kns = [], [], [], []
    k_raw = []
    for kh in range(B_KV_HEADS):
        lo, hi = kh * B_HD, (kh + 1) * B_HD
        k_raw.append(jnp.concatenate([kp_ref[:, lo:hi], ko_ref[:, lo:hi]], axis=0))
        vs.append(jnp.concatenate([vp_ref[:, lo:hi], vo_ref[:, lo:hi]], axis=0).astype(BF16))
    for kn in _rms_all(k_raw, gk):
        kns.append(kn[nq * W:])
        ks.append(kn.astype(BF16))
    q_raw = [q_ref[j * W:(j + 1) * W, h * B_HD:(h + 1) * B_HD]
             for j, kh in units for h in range(kh * B_GROUP, (kh + 1) * B_GROUP)]
    q_n = _rms_all(q_raw, gq)
    qs = [jnp.concatenate(q_n[u * B_GROUP:(u + 1) * B_GROUP], axis=0).astype(BF16)
          for u in range(len(units))]
    scores = [_dot_nt(q, ks[kh][j * W:(j + 2) * W]) for q, (j, kh) in zip(qs, units)]
    logits, sinks = [], []
    for s_all, (j, kh) in zip(scores, units):
        valid = in_band_first if j == 0 else in_band
        for g in range(B_GROUP):
            h = kh * B_GROUP + g
            s = s_all[g * W:(g + 1) * W] * (B_HD ** -0.5)
            logits.append(jnp.where(valid, s - _alibi_slope(h) * dist_f, NEG_INF))
            sinks.append(sink_ref[h])
    pieces = [p.astype(BF16) for p in _sink_softmax_all(logits, sinks)]
    probs = [jnp.concatenate(pieces[u * B_GROUP:(u + 1) * B_GROUP], axis=0) for u in range(len(units))]
    outs = [_dot(p, vs[kh][j * W:(j + 2) * W]) for p, (j, kh) in zip(probs, units)]
    for j in range(nq):
        o_ref[j * W:(j + 1) * W, :] = jnp.concatenate(
            [outs[j * B_KV_HEADS + kh][g * W:(g + 1) * W]
             for kh in range(B_KV_HEADS) for g in range(B_GROUP)], axis=-1).astype(o_ref.dtype)
    kn_ref[0] = jnp.concatenate(kns, axis=-1)


def _swa_prompt(z, g_q, g_k, sinks, *, n_seq, seq_len, nq):
    W = WINDOW
    nb = seq_len // (nq * W)
    bq = B_HEADS * B_HD
    bkv = B_KV_HEADS * B_HD
    kcol = bq // bkv

    def own(col):
        return lambda b, i: (b * nb + i, col)

    def prev(col):
        return lambda b, i: (jnp.maximum((b * nb + i) * nq - 1, 0), col)

    return pl.pallas_call(
        functools.partial(_swa_prompt_kernel, nq=nq),
        grid=(n_seq, nb),
        in_specs=[pl.BlockSpec(memory_space=pltpu.SMEM),
                  pl.BlockSpec((nq * W, bq), own(0)),
                  pl.BlockSpec((W, bkv), prev(kcol)),
                  pl.BlockSpec((nq * W, bkv), own(kcol)),
                  pl.BlockSpec((W, bkv), prev(kcol + 1)),
                  pl.BlockSpec((nq * W, bkv), own(kcol + 1)),
                  pl.BlockSpec((1, B_HD), lambda b, i: (0, 0)),
                  pl.BlockSpec((1, B_HD), lambda b, i: (0, 0))],
        out_specs=(pl.BlockSpec((nq * W, bq), own(0)),
                   pl.BlockSpec((1, W, bkv), lambda b, i: (b, 0, 0))),
        out_shape=(jax.ShapeDtypeStruct((n_seq * seq_len, bq), BF16),
                   jax.ShapeDtypeStruct((n_seq, W, bkv), F32)),
        compiler_params=_params("parallel", "arbitrary"),
        name="swa_prompt",
    )(sinks, z, z, z, z, z, g_q.reshape(1, B_HD), g_k.reshape(1, B_HD))


def _swa_sample_kernel(sink_ref, q_ref, k_ref, v_ref, ck_ref, cv_ref, gq_ref, gk_ref,
                       o_ref, cko_ref, cvo_ref, *, n_new, n_batch):
    W = WINDOW
    T = n_new
    R = B_HEADS * T
    GT = B_GROUP * T
    r_col = lax.broadcasted_iota(jnp.int32, (R, 1), 0)
    s_idx = lax.broadcasted_iota(jnp.int32, (R, W + T), 1)
    t_col = r_col
    slope_col = jnp.full((R, 1), _alibi_slope(0), F32)
    sink_col = jnp.full((R, 1), sink_ref[0], F32)
    for h in range(1, B_HEADS):
        in_later_head = r_col >= h * T
        t_col = jnp.where(in_later_head, r_col - h * T, t_col)
        slope_col = jnp.where(in_later_head, _alibi_slope(h), slope_col)
        sink_col = jnp.where(in_later_head, sink_ref[h], sink_col)
    dist = W + t_col - s_idx
    valid = (dist >= 0) & (dist <= W)
    bias = slope_col * dist.astype(F32)
    gq = gq_ref[...]
    gk = gk_ref[...]
    qn = _rms_all([q_ref[:, h * B_HD:(h + 1) * B_HD] for h in range(B_HEADS)], gq)
    kn = _rms_all([k_ref[:, kh * B_HD:(kh + 1) * B_HD] for kh in range(B_KV_HEADS)], gk)
    qs, ks, vs = [], [], []
    for bi in range(n_batch):
        r0, r1 = bi * T, (bi + 1) * T
        for kh in range(B_KV_HEADS):
            lo, hi = kh * B_HD, (kh + 1) * B_HD
            k_all = jnp.concatenate([ck_ref[bi, :, lo:hi], kn[kh][r0:r1]], axis=0)
            v_all = jnp.concatenate([cv_ref[bi, :, lo:hi], v_ref[r0:r1, lo:hi]], axis=0)
            cko_ref[bi, :, lo:hi] = k_all[T:]
            cvo_ref[bi, :, lo:hi] = v_all[T:]
            ks.append(k_all)
            vs.append(v_all)
            heads = range(kh * B_GROUP, (kh + 1) * B_GROUP)
            qs.append(jnp.concatenate([qn[h][r0:r1] for h in heads], axis=0))
    scores = [_dot_nt(q, k) for q, k in zip(qs, ks)]
    logits = []
    for bi in range(n_batch):
        s = jnp.concatenate(scores[bi * B_KV_HEADS:(bi + 1) * B_KV_HEADS], axis=0)
        logits.append(jnp.where(valid, s * (B_HD ** -0.5) - bias, NEG_INF))
    probs = _sink_softmax_all(logits, [sink_col] * n_batch)
    outs = [_dot(probs[i // B_KV_HEADS][(i % B_KV_HEADS) * GT:(i % B_KV_HEADS + 1) * GT], v)
            for i, v in enumerate(vs)]
    rows_out = []
    for bi in range(n_batch):
        rows_out.append(jnp.concatenate(
            [outs[bi * B_KV_HEADS + kh][g * T:(g + 1) * T]
             for kh in range(B_KV_HEADS) for g in range(B_GROUP)], axis=-1))
    o_ref[...] = jnp.concatenate(rows_out, axis=0).astype(o_ref.dtype)


def _swa_sample(z, cache_k, cache_v, g_q, g_k, sinks, *, row0, n_new, n_batch):
    W = WINDOW
    n_seq = cache_k.shape[0]
    bq = B_HEADS * B_HD
    bkv = B_KV_HEADS * B_HD
    kcol = bq // bkv
    rows = n_new * n_batch
    rb0 = row0 // rows
    ck = cache_k.reshape(n_seq, W, bkv)
    cv = cache_v.reshape(n_seq, W, bkv)
    cache_spec = pl.BlockSpec((n_batch, W, bkv), lambda i: (i, 0, 0))
    o, cko, cvo = pl.pallas_call(
        functools.partial(_swa_sample_kernel, n_new=n_new, n_batch=n_batch),
        grid=(n_seq // n_batch,),
        in_specs=[pl.BlockSpec(memory_space=pltpu.SMEM),
                  pl.BlockSpec((rows, bq), lambda i: (rb0 + i, 0)),
                  pl.BlockSpec((rows, bkv), lambda i: (rb0 + i, kcol)),
                  pl.BlockSpec((rows, bkv), lambda i: (rb0 + i, kcol + 1)),
                  cache_spec, cache_spec,
                  pl.BlockSpec((1, B_HD), lambda i: (0, 0)),
                  pl.BlockSpec((1, B_HD), lambda i: (0, 0))],
        out_specs=(pl.BlockSpec((rows, bq), lambda i: (i, 0)), cache_spec, cache_spec),
        out_shape=(jax.ShapeDtypeStruct((n_seq * n_new, bq), BF16),
                   jax.ShapeDtypeStruct(ck.shape, cache_k.dtype),
                   jax.ShapeDtypeStruct(cv.shape, cache_v.dtype)),
        compiler_params=_params("parallel"),
        name="swa_sample",
    )(sinks, z, z, z, ck, cv, g_q.reshape(1, B_HD), g_k.reshape(1, B_HD))
    return o, cko.reshape(cache_k.shape), cvo.reshape(cache_v.shape)


def _top_ranked(s, k):
    rows = s.shape[0]
    ridx = lax.broadcasted_iota(jnp.int32, s.shape, 0).astype(F32)
    rank = jnp.full(s.shape, float(k), F32)
    vals = []
    for r in range(k):
        m = jnp.max(s, axis=0, keepdims=True)
        vals.append(m)
        first = ridx == jnp.min(jnp.where(s == m, ridx, float(rows)), axis=0, keepdims=True)
        rank = jnp.where(first, float(r), rank)
        s = jnp.where(first, NEG_INF, s)
    return vals, rank


def _top_ranked_distinct(s, k):
    rank = jnp.full(s.shape, float(k), F32)
    vals = []
    for r in range(k):
        m = jnp.max(s, axis=0, keepdims=True)
        vals.append(m)
        hit = s == m
        rank = jnp.where(hit, float(r), rank)
        s = jnp.where(hit, NEG_INF, s)
    return vals, rank


def _ranked_ok(rank, k):
    n_ranked = jnp.sum(jnp.where(rank < k, 1.0, 0.0), axis=0, keepdims=True)
    return jnp.where(n_ranked == k, 1.0, 0.0)


_CAND_PAIRS = [(a, b) for a in range(P_TOPK) for b in range(P_TOPK) if (a + 1) * (b + 1) <= P_TOPK]
_CAND_ROWS = -(-len(_CAND_PAIRS) // 8) * 8


TAB_COUNT0, TAB_GATE0 = 0, 1
TAB_RANK1, TAB_GATE1 = 0, 1


def _peer_route_kernel(x_ref, g_ref, wqt_ref, keys_ref, xnt_ref, tab0_ref, tab1_ref,
                       qt_ref, cand_ref, tied_ref):
    xn = _rms(x_ref[...], g_ref[...])
    xnt = xn.T.astype(BF16)
    xnt_ref[...] = xnt
    qt_ref[...] = _dot(wqt_ref[...], xnt)
    half = keys_ref.shape[2]
    cand_ref[...] = jnp.full(cand_ref.shape, NEG_INF, F32)

    def route_head(h, top_fn):
        row0 = 2 * h * half if isinstance(h, int) else pl.multiple_of(2 * h * half, half)
        q0 = qt_ref[pl.ds(row0, half), :]
        q1 = qt_ref[pl.ds(row0 + half, half), :]
        s0 = _dot(keys_ref[2 * h], q0)
        s1 = _dot(keys_ref[2 * h + 1], q1)
        top0, rank0 = top_fn(s0, P_TOPK)
        top1, rank1 = top_fn(s1, P_TOPK)
        for r, (a, b) in enumerate(_CAND_PAIRS):
            cand_ref[r:r + 1, :] = top0[a] + top1[b]
        best, cand_rank = top_fn(cand_ref[...], P_TOPK)
        z = jnp.zeros_like(best[0])
        for c in best:
            z = z + jnp.exp(c - best[0])
        taken = jnp.where(cand_rank < P_TOPK, 1.0, 0.0)
        count0 = jnp.zeros_like(s0)
        for a in range(P_TOPK):
            rows = [r for r, (ca, _) in enumerate(_CAND_PAIRS) if ca == a]
            n_sel = jnp.sum(taken[rows[0]:rows[-1] + 1], axis=0, keepdims=True)
            count0 = jnp.where(rank0 == a, n_sel, count0)
        tab0_ref[2 * h + TAB_COUNT0] = 2.0 * count0 - 1.0
        tab0_ref[2 * h + TAB_GATE0] = jnp.exp(s0 - top0[0]) / z
        tab1_ref[2 * h + TAB_RANK1] = (2.0 * rank1).astype(BF16)
        tab1_ref[2 * h + TAB_GATE1] = jnp.exp(s1 - top1[0]).astype(BF16)
        return _ranked_ok(rank0, P_TOPK) * _ranked_ok(rank1, P_TOPK) * _ranked_ok(cand_rank, P_TOPK)

    for h in range(P_HEADS):
        ok = route_head(h, _top_ranked_distinct)
        tied_ref[h] = (jnp.min(ok) < 1.0).astype(jnp.int32)

    def redo(h, carry):
        @pl.when(tied_ref[h] != 0)
        def _():
            route_head(h, _top_ranked)
        return carry

    lax.fori_loop(0, P_HEADS, redo, 0)


def _peer_route(x, g, wqt, keys, *, tt):
    t, d = x.shape
    nq = wqt.shape[0]
    hp, nk, half = keys.shape
    tab_spec = pl.BlockSpec((2 * P_HEADS, nk, tt), lambda i: (0, 0, i))
    return pl.pallas_call(
        _peer_route_kernel,
        grid=(t // tt,),
        in_specs=[pl.BlockSpec((tt, d), lambda i: (i, 0)),
                  pl.BlockSpec((1, d), lambda i: (0, 0)),
                  pl.BlockSpec((nq, d), lambda i: (0, 0)),
                  pl.BlockSpec((hp, nk, half), lambda i: (0, 0, 0))],
        out_specs=(pl.BlockSpec((d, tt), lambda i: (0, i)), tab_spec, tab_spec),
        out_shape=(jax.ShapeDtypeStruct((d, t), BF16),
                   jax.ShapeDtypeStruct((2 * P_HEADS, nk, t), F32),
                   jax.ShapeDtypeStruct((2 * P_HEADS, nk, t), BF16)),
        scratch_shapes=[pltpu.VMEM((nq, tt), F32), pltpu.VMEM((_CAND_ROWS, tt), F32),
                        pltpu.SMEM((P_HEADS,), jnp.int32)],
        compiler_params=_params("parallel"),
        name="peer_route",
    )(x, g.reshape(1, d), wqt, keys)


def _gelu(x):
    return 0.5 * x * (1.0 + lax.erf(x * math.sqrt(0.5)))


def _rows_bf16(row, n):
    packed = jnp.broadcast_to(row, (BF16_SUBLANES, LANES)).astype(BF16)
    return jnp.concatenate([packed] * (n // BF16_SUBLANES), axis=0)


def _peer_expert_kernel(xnt_ref, tab0_ref, tab1_ref, u_ref, vt_ref, yt_ref, w0_ref, *, ne, nc):
    et = pl.program_id(1)
    n_et = pl.num_programs(1)
    tt = xnt_ref.shape[1]
    n_i = ne // N_KEYS
    assert n_i == SUBLANES and nc % LANES == 0
    n_chunk = tt // nc
    row_group = 4
    blocks_rc = [(r, c) for r in range(n_i) for c in range(nc // LANES)]

    def hidden(k):
        return _dot(u_ref[...], xnt_ref[:, k * nc:(k + 1) * nc])

    def gates(k, tile):
        rows8 = pl.ds(pl.multiple_of(tile * n_i, SUBLANES), SUBLANES)
        blocks = {}
        for c in range(nc // LANES):
            cs = slice(k * nc + c * LANES, k * nc + (c + 1) * LANES)
            for r0 in range(0, n_i, row_group):
                rows = range(r0, r0 + row_group)
                w = {r: jnp.zeros((N_KEYS, LANES), BF16) for r in rows}
                for h in range(P_HEADS):
                    rank1 = tab1_ref[2 * h + TAB_RANK1, :, cs]
                    gate1 = tab1_ref[2 * h + TAB_GATE1, :, cs]
                    count0 = tab0_ref[2 * h + TAB_COUNT0, rows8, cs]
                    gate0 = tab0_ref[2 * h + TAB_GATE0, rows8, cs]
                    for r in rows:
                        sel_gate1 = jnp.minimum(
                            jnp.maximum(_rows_bf16(count0[r:r + 1], N_KEYS) - rank1, 0.0), gate1)
                        w[r] = w[r] + sel_gate1 * _rows_bf16(gate0[r:r + 1], N_KEYS)
                for r in rows:
                    blocks[r, c] = w[r]
        return blocks

    def activate(hid, w):
        return jnp.concatenate(
            [jnp.concatenate(
                [_gelu(hid[r * N_KEYS:(r + 1) * N_KEYS, c * LANES:(c + 1) * LANES]).astype(BF16) * w[r, c]
                 for c in range(nc // LANES)], axis=1)
             for r in range(n_i)], axis=0)

    def accumulate(k, a):
        yt_ref[:, k * nc:(k + 1) * nc] += _dot(vt_ref[...], a)

    def store_gates(w):
        for r, c in blocks_rc:
            w0_ref[r * N_KEYS:(r + 1) * N_KEYS, c * LANES:(c + 1) * LANES] = w[r, c]

    @pl.when(et == 0)
    def _():
        yt_ref[...] = jnp.zeros_like(yt_ref)
        store_gates(gates(0, et))

    w = {(r, c): w0_ref[r * N_KEYS:(r + 1) * N_KEYS, c * LANES:(c + 1) * LANES] for r, c in blocks_rc}
    hid = hidden(0)
    a_prev = None
    for k in range(n_chunk):
        last = k + 1 == n_chunk
        hid_next = None if last else hidden(k + 1)
        if a_prev is not None:
            accumulate(k - 1, a_prev)
        w_next = None if last else gates(k + 1, et)
        a_prev = activate(hid, w)
        hid, w = hid_next, w_next
    w_ahead = gates(0, jnp.minimum(et + 1, n_et - 1))
    accumulate(n_chunk - 1, a_prev)
    store_gates(w_ahead)


def _peer_experts(xnt, tab0, tab1, u, vt, *, tt, ne, nc):
    d, t = xnt.shape
    n_tab, nk, _ = tab0.shape
    n_exp = u.shape[0]
    once = pl.Buffered(1)
    tab_spec = pl.BlockSpec((n_tab, nk, tt), lambda i, e: (0, 0, i), pipeline_mode=once)
    return pl.pallas_call(
        functools.partial(_peer_expert_kernel, ne=ne, nc=nc),
        grid=(t // tt, n_exp // ne),
        in_specs=[pl.BlockSpec((d, tt), lambda i, e: (0, i), pipeline_mode=once),
                  tab_spec, tab_spec,
                  pl.BlockSpec((ne, d), lambda i, e: (e, 0)),
                  pl.BlockSpec((d, ne), lambda i, e: (0, e))],
        out_specs=pl.BlockSpec((d, tt), lambda i, e: (0, i), pipeline_mode=once),
        out_shape=jax.ShapeDtypeStruct((d, t), F32),
        scratch_shapes=[pltpu.VMEM((ne, nc), BF16)],
        compiler_params=_params("parallel", "arbitrary"),
        name="peer_experts",
    )(xnt, tab0, tab1, u, vt)


def _add_transposed_kernel(x_ref, yt_ref, o_ref):
    o_ref[...] = x_ref[...] + yt_ref[...].T


def _add_transposed(x, yt, *, tt, row0=0, rows=None):
    t, d = x.shape
    rows = t if rows is None else rows
    b0 = row0 // tt
    return pl.pallas_call(
        _add_transposed_kernel,
        grid=(rows // tt,),
        in_specs=[pl.BlockSpec((tt, d), lambda i: (b0 + i, 0)),
                  pl.BlockSpec((d, tt), lambda i: (0, b0 + i))],
        out_specs=pl.BlockSpec((tt, d), lambda i: (i, 0)),
        out_shape=jax.ShapeDtypeStruct((rows, d), F32),
        compiler_params=_params("parallel"),
        name="add_transposed",
    )(x, yt)


def _cast_kernel(x_ref, o_ref, *, transpose):
    x = x_ref[...]
    o_ref[...] = (x.T if transpose else x).astype(o_ref.dtype)


def _layer_cast(w, layer, dtype, *, rows, transpose):
    _, n, d = w.shape
    return pl.pallas_call(
        functools.partial(_cast_kernel, transpose=transpose),
        grid=(n // rows,),
        in_specs=[pl.BlockSpec((None, rows, d), lambda i: (layer, i, 0))],
        out_specs=(pl.BlockSpec((d, rows), lambda i: (0, i)) if transpose
                   else pl.BlockSpec((rows, d), lambda i: (i, 0))),
        out_shape=jax.ShapeDtypeStruct((d, n) if transpose else (n, d), dtype),
        compiler_params=_params("parallel"),
        name="layer_cast_t" if transpose else "layer_cast",
    )(w)


def _peer(x, g, peer_wq, keys, peer_u, peer_v, layer, *, tt_route, tt, ne):
    hp = keys.shape[0] * keys.shape[1]
    wqt = _layer_cast(peer_wq, layer, BF16, rows=512, transpose=True)
    xnt, tab0, tab1 = _peer_route(x, g, wqt, keys.reshape(hp, keys.shape[2], keys.shape[3]), tt=tt_route)
    u = _layer_cast(peer_u, layer, BF16, rows=512, transpose=False)
    vt = _layer_cast(peer_v, layer, BF16, rows=512, transpose=True)
    return _peer_experts(xnt, tab0, tab1, u, vt, tt=tt, ne=ne, nc=min(tt, 4 * LANES))


def _pick_tile(n, pref):
    t = pref
    while n % t:
        t //= 2
    return t


def kernel(x_prompt, x_sample, state_c, state_n, state_m, cache_k, cache_v, norm_mix, w_in_a, b_gate_a,
           norm_h_a, w_out_a, w_in_b, g_q_b, g_k_b, sink_b, w_out_b, norm_ffn, peer_wq, peer_keys,
           peer_u, peer_v):
    bp, sp, d = x_prompt.shape
    bs, ss, _ = x_sample.shape
    tp, ts = bp * sp, bs * ss
    t = tp + ts
    x = jnp.concatenate([x_prompt.reshape(tp, d), x_sample.reshape(ts, d)], axis=0)
    tm = _pick_tile(math.gcd(tp, ts), 1024)
    tr = min(tm, 256)
    te = _pick_tile(t, 1024)
    n_qkvo = 2 * A_HEADS * (A_DK + A_DV)
    chunk_p = _pick_tile(sp, 256)

    w_in = w_in_a[0]
    w_gate = jnp.pad(w_in[:, n_qkvo:], ((0, 0), (0, LANES - 2 * A_HEADS)))
    z, zg = _norm_matmul_gates(x, norm_mix[0], w_in[:, :n_qkvo].astype(BF16), w_gate, tm=tm, tn=1024)
    h_p, c_p, n_p, m_p = _mlstm(z, zg, b_gate_a[0], norm_h_a[0], row0=0, n_seq=bp, seq_len=sp,
                                chunk=chunk_p, n_sub=1)
    h_s, c_s, n_s, m_s = _mlstm(z, zg, b_gate_a[0], norm_h_a[0], row0=tp, n_seq=bs, seq_len=ss,
                                chunk=ss, n_sub=4, state=(state_c[0], state_n[0], state_m[0]))
    x = _matmul_res(h_p, h_s, w_out_a[0].astype(BF16), x, tm=tm, tn=1024)
    yt = _peer(x, norm_ffn[0], peer_wq, peer_keys[0], peer_u, peer_v, 0, tt_route=tr, tt=te, ne=1024)
    x = _add_transposed(x, yt, tt=tr)

    z = _norm_matmul(x, norm_mix[1], w_in_b[0].astype(BF16), tm=tm, tn=1024)
    o_p, kn_p = _swa_prompt(z, g_q_b[0], g_k_b[0], sink_b[0], n_seq=bp, seq_len=sp,
                            nq=2 if sp % (2 * WINDOW) == 0 else 1)
    o_s, ck_s, cv_s = _swa_sample(z, cache_k[0], cache_v[0], g_q_b[0], g_k_b[0], sink_b[0],
                                  row0=tp, n_new=ss, n_batch=8)
    x = _matmul_res(o_p, o_s, w_out_b[0].astype(BF16), x, tm=tm, tn=1024)
    yt = _peer(x, norm_ffn[1], peer_wq, peer_keys[1], peer_u, peer_v, 1, tt_route=tr, tt=te, ne=1024)
    y_p = _add_transposed(x, yt, tt=tr, row0=0, rows=tp)
    y_s = _add_transposed(x, yt, tt=tr, row0=tp, rows=ts)

    bq = B_HEADS * B_HD
    bkv = B_KV_HEADS * B_HD
    v_p = z[:tp, bq + bkv:].reshape(bp, sp, bkv)[:, sp - WINDOW:]
    cache_shape = (1, bp, WINDOW, B_KV_HEADS, B_HD)
    return (y_p.reshape(bp, sp, d), y_s.reshape(bs, ss, d),
            c_p[None], n_p[None], m_p[None],
            kn_p.reshape(cache_shape), v_p.reshape(cache_shape),
            c_s[None], n_s[None], m_s[None], ck_s[None], cv_s[None])
```
